```python
import math
import jax, jax.numpy as jnp
from jax import lax
import numpy as np

D_MODEL = 2048
BATCH = 2
SEQ = 8192
DEPTH = 4

N_A = DEPTH // 2
N_B = DEPTH - N_A
N_HEADS = 16
HEAD_DIM = 128
N_KV = 4
GROUP = N_HEADS // N_KV
IDX_HEADS = 16
IDX_DIM = 64
TOPK_MAX = 256
WINDOW = 128
BLOCK = 128
N_BUCKETS = 32
MAX_DIST = 128
D_FF = ((8 * D_MODEL + 3 * 256 - 1) // (3 * 256)) * 256
PLE_DIM = 256
EPS = 1e-6
NEG = -1e30

A_Q = N_HEADS * HEAD_DIM
A_KV = N_KV * HEAD_DIM
A_QI = IDX_HEADS * IDX_DIM
A_IN = A_Q + 2 * A_KV + A_QI + IDX_DIM + IDX_HEADS
A_SPLITS = [A_Q, A_Q + A_KV, A_Q + 2 * A_KV, A_Q + 2 * A_KV + A_QI, A_Q + 2 * A_KV + A_QI + IDX_DIM]

kernel_name = "yoco_dsa_swa_sink_hybrid"


def rmsnorm(x, g):
    xf = x.astype(jnp.float32)
    y = xf * lax.rsqrt(jnp.mean(xf * xf, axis=-1, keepdims=True) + EPS)
    return (y * g.astype(jnp.float32)).astype(x.dtype)


def t5_bucket(rel):
    n = jnp.maximum(rel, 0)
    max_exact = N_BUCKETS // 2
    nf = jnp.maximum(n, 1).astype(jnp.float32)
    large = max_exact + (jnp.log(nf / max_exact) / math.log(MAX_DIST / max_exact)
                         * (N_BUCKETS - max_exact)).astype(jnp.int32)
    large = jnp.minimum(large, N_BUCKETS - 1)
    return jnp.where(n < max_exact, n, large)


def swiglu(h, w1, w3, w2):
    return (jax.nn.silu(h @ w1) * (h @ w3)) @ w2


def dsa_attention(h, w_in, w_out, rel_bias):
    B, S, _ = h.shape
    nb = S // BLOCK
    topk = min(TOPK_MAX, S // 4)
    proj = h @ w_in
    q, k, v, qi, ki, wi = jnp.split(proj, A_SPLITS, axis=-1)
    q = q.reshape(B, nb, BLOCK, N_KV, GROUP, HEAD_DIM).swapaxes(0, 1)
    k = k.reshape(B, S, N_KV, HEAD_DIM)
    v = v.reshape(B, S, N_KV, HEAD_DIM)
    qi = qi.reshape(B, nb, BLOCK, IDX_HEADS, IDX_DIM).swapaxes(0, 1)
    wi = wi.reshape(B, nb, BLOCK, IDX_HEADS).swapaxes(0, 1)
    key_pos = jnp.arange(S)

    def block_fn(args):
        j, qb, qib, wib = args
        t = j * BLOCK + jnp.arange(BLOCK)
        sc = jnp.einsum('bqhd,bsd->bqhs', qib, ki).astype(jnp.float32) * (IDX_DIM ** -0.5)
        w = wib.astype(jnp.float32) * (IDX_HEADS ** -0.5)
        score = jnp.einsum('bqhs,bqh->bqs', jax.nn.relu(sc), w)
        causal = key_pos[None, :] <= t[:, None]
        score = jnp.where(causal[None], score, NEG)
        _, idx = lax.top_k(score, topk)
        ks = jax.vmap(lambda kb, ib: kb[ib])(k, idx)
        vs = jax.vmap(lambda vb, ib: vb[ib])(v, idx)
        logits = jnp.einsum('bqgrd,bqkgd->bqgrk', qb, ks).astype(jnp.float32) * (HEAD_DIM ** -0.5)
        rel = t[None, :, None] - idx
        bias = rel_bias[t5_bucket(rel)].astype(jnp.float32)
        bias = bias.reshape(B, BLOCK, topk, N_KV, GROUP).transpose(0, 1, 3, 4, 2)
        valid = (rel >= 0)[:, :, None, None, :]
        logits = jnp.where(valid, logits + bias, NEG)
        probs = jax.nn.softmax(logits, axis=-1).astype(vs.dtype)
        return jnp.einsum('bqgrk,bqkgd->bqgrd', probs, vs)

    out = lax.map(block_fn, (jnp.arange(nb), q, qi, wi))
    out = out.swapaxes(0, 1).reshape(B, S, A_Q)
    return out @ w_out


def band_blocks(a):
    B, S = a.shape[0], a.shape[1]
    ab = a.reshape(B, S // BLOCK, BLOCK, N_KV, HEAD_DIM)
    prev = jnp.pad(ab[:, :-1], ((0, 0), (1, 0), (0, 0), (0, 0), (0, 0)))
    return jnp.concatenate([prev, ab], axis=2)


def swa_attention(h, w_q, w_out, sinks, k_band, v_band, bias_band, valid):
    B, S, _ = h.shape
    nb = S // BLOCK
    q = (h @ w_q).reshape(B, nb, BLOCK, N_KV, GROUP, HEAD_DIM)
    logits = jnp.einsum('bnqgrd,bnkgd->bngrqk', q, k_band).astype(jnp.float32) * (HEAD_DIM ** -0.5)
    logits = jnp.where(valid[None, :, None, None], logits + bias_band[None, None], NEG)
    sink = jnp.broadcast_to(sinks.astype(jnp.float32).reshape(N_KV, GROUP, 1, 1), logits.shape[:-1] + (1,))
    probs = jax.nn.softmax(jnp.concatenate([logits, sink], axis=-1), axis=-1)[..., :-1]
    out = jnp.einsum('bngrqk,bnkgd->bnqgrd', probs.astype(v_band.dtype), v_band)
    return out.reshape(B, S, A_Q) @ w_out


def setup_inputs(seed: int = 0) -> dict:
    key = jax.random.key(seed)
    ks = jax.random.split(key, 20)
    f32 = jnp.float32

    def w(k, shape, fan_in):
        return jax.random.normal(k, shape, f32) * (fan_in ** -0.5)

    def gain(k, shape):
        return 1.0 + 0.02 * jax.random.normal(k, shape, f32)

    return {
        "x": jax.random.normal(ks[0], (BATCH, SEQ, D_MODEL), f32),
        "p": jax.random.normal(ks[1], (DEPTH, BATCH, SEQ, PLE_DIM), f32),
        "w_in_a": w(ks[2], (N_A, D_MODEL, A_IN), D_MODEL),
        "w_out_a": w(ks[3], (N_A, A_Q, D_MODEL), A_Q),
        "w_q_b": w(ks[4], (N_B, D_MODEL, A_Q), D_MODEL),
        "w_out_b": w(ks[5], (N_B, A_Q, D_MODEL), A_Q),
        "sinks": jax.random.normal(ks[6], (N_B, N_HEADS), f32),
        "g_kv": gain(ks[7], (D_MODEL,)),
        "w_kv": w(ks[8], (D_MODEL, 2 * A_KV), D_MODEL),
        "rel_bias": 0.5 * jax.random.normal(ks[9], (N_BUCKETS, N_HEADS), f32),
        "g_attn": gain(ks[10], (DEPTH, D_MODEL)),
        "g_ffn": gain(ks[11], (DEPTH, D_MODEL)),
        "w1": w(ks[12], (DEPTH, D_MODEL, D_FF), D_MODEL),
        "w3": w(ks[13], (DEPTH, D_MODEL, D_FF), D_MODEL),
        "w2": w(ks[14], (DEPTH, D_FF, D_MODEL), D_FF),
        "g_pe": gain(ks[15], (DEPTH, D_MODEL)),
        "w_pe": w(ks[16], (DEPTH, PLE_DIM, D_MODEL), PLE_DIM),
        "w_pg": w(ks[17], (DEPTH, D_MODEL, D_MODEL), D_MODEL),
        "g_final": gain(ks[18], (D_MODEL,)),
    }


def reference(x, p, w_in_a, w_out_a, w_q_b, w_out_b, sinks, g_kv, w_kv, rel_bias,
              g_attn, g_ffn, w1, w3, w2, g_pe, w_pe, w_pg, g_final):
    B, S, _ = x.shape
    nb = S // BLOCK
    qi_ = jnp.arange(BLOCK)
    ci_ = jnp.arange(2 * BLOCK)
    rel = qi_[:, None] + BLOCK - ci_[None, :]
    in_win = (rel >= 0) & (rel < WINDOW)
    key_abs = jnp.arange(nb)[:, None, None] * BLOCK - BLOCK + ci_[None, None, :]
    valid = in_win[None] & (key_abs >= 0)
    bias_band = rel_bias[t5_bucket(rel)].astype(jnp.float32)
    bias_band = bias_band.reshape(BLOCK, 2 * BLOCK, N_KV, GROUP).transpose(2, 3, 0, 1)

    k_band = None
    v_band = None
    for i in range(DEPTH):
        h = rmsnorm(x, g_attn[i])
        if i < N_A:
            x = x + dsa_attention(h, w_in_a[i], w_out_a[i], rel_bias)
        else:
            j = i - N_A
            x = x + swa_attention(h, w_q_b[j], w_out_b[j], sinks[j], k_band, v_band, bias_band, valid)
        x = x + swiglu(rmsnorm(x, g_ffn[i]), w1[i], w3[i], w2[i])
        gate = jax.nn.sigmoid(rmsnorm(x, g_pe[i]) @ w_pg[i])
        x = x + (p[i] @ w_pe[i]) * gate
        if i == N_A - 1:
            kv = rmsnorm(x, g_kv) @ w_kv
            k_s, v_s = jnp.split(kv, 2, axis=-1)
            k_band = band_blocks(k_s.reshape(B, S, N_KV, HEAD_DIM))
            v_band = band_blocks(v_s.reshape(B, S, N_KV, HEAD_DIM))
    return rmsnorm(x, g_final)
```

```python
import functools
import math

import numpy as np
import jax
import jax.numpy as jnp
from jax import lax
from jax.experimental import pallas as pl
from jax.experimental.pallas import tpu as pltpu

F32 = jnp.float32
BF16 = jnp.bfloat16
I32 = jnp.int32

N_HEADS = 16
HEAD_DIM = 128
N_KV = 4
GROUP = N_HEADS // N_KV
IDX_HEADS = 16
IDX_DIM = 64
TOPK_MAX = 256
WINDOW = 128
N_BUCKETS = 32
MAX_DIST = 128
EPS = 1e-6
NEG = -1e30
INT_MIN = -(2 ** 31)

V7X_VMEM_BYTES = 64 * 1024 * 1024
VMEM_LIMIT = V7X_VMEM_BYTES - 8 * 1024 * 1024
LANES = 128

TM = 512
TF = 512
DSA_T = 256
SWA_TQ = 128


def _params(*sem):
    return pltpu.CompilerParams(dimension_semantics=sem, vmem_limit_bytes=VMEM_LIMIT)


def _resident(shape, index_map):
    return pl.BlockSpec(shape, index_map, pipeline_mode=pl.Buffered(1))


def _rms(x, g):
    ms = jnp.mean(x * x, axis=-1, keepdims=True)
    return x * lax.rsqrt(ms + EPS) * g


def _t5_bucket_np(d):
    d = np.maximum(d, 0)
    max_exact = N_BUCKETS // 2
    nf = np.maximum(d, 1).astype(np.float32)
    large = max_exact + (np.log(nf / np.float32(max_exact)) / np.float32(math.log(MAX_DIST / max_exact))
                         * np.float32(N_BUCKETS - max_exact)).astype(np.int32)
    large = np.minimum(large, N_BUCKETS - 1)
    return np.where(d < max_exact, d, large)


def _norm_linear_kernel(*refs, scaled):
    if scaled:
        x_ref, g_ref, w_ref, s_ref, o_ref, h_ref = refs
    else:
        x_ref, g_ref, w_ref, o_ref, h_ref = refs

    @pl.when(pl.program_id(1) == 0)
    def _():
        h_ref[...] = _rms(x_ref[...], g_ref[...]).astype(BF16)

    y = jnp.dot(h_ref[...], w_ref[...], preferred_element_type=F32)
    if scaled:
        y = y * s_ref[...]
    o_ref[...] = y.astype(o_ref.dtype)


def norm_linear(x, g, w, col_scale, out_dtype, tn):
    t, d = x.shape
    n = w.shape[1]
    scaled = col_scale is not None
    in_specs = [
        pl.BlockSpec((TM, d), lambda i, j: (i, 0)),
        pl.BlockSpec((1, d), lambda i, j: (0, 0)),
        pl.BlockSpec((d, tn), lambda i, j: (0, j)),
    ]
    args = [x, g.reshape(1, d), w]
    if scaled:
        in_specs.append(pl.BlockSpec((1, tn), lambda i, j: (0, j)))
        args.append(col_scale.reshape(1, n))
    return pl.pallas_call(
        functools.partial(_norm_linear_kernel, scaled=scaled),
        out_shape=jax.ShapeDtypeStruct((t, n), out_dtype),
        grid=(t // TM, n // tn),
        in_specs=in_specs,
        out_specs=pl.BlockSpec((TM, tn), lambda i, j: (i, j)),
        scratch_shapes=[pltpu.VMEM((TM, d), BF16)],
        compiler_params=_params("parallel", "arbitrary"),
        name="norm_linear",
    )(*args)


def _linear_residual_kernel(a_ref, w_ref, r_ref, o_ref):
    o_ref[...] = r_ref[...] + jnp.dot(a_ref[...], w_ref[...], preferred_element_type=F32)


def linear_residual(a, w, res):
    t, k = a.shape
    n = w.shape[1]
    return pl.pallas_call(
        _linear_residual_kernel,
        out_shape=jax.ShapeDtypeStruct((t, n), F32),
        grid=(t // TM,),
        in_specs=[
            pl.BlockSpec((TM, k), lambda i: (i, 0)),
            _resident((k, n), lambda i: (0, 0)),
            pl.BlockSpec((TM, n), lambda i: (i, 0)),
        ],
        out_specs=pl.BlockSpec((TM, n), lambda i: (i, 0)),
        compiler_params=_params("parallel"),
        name="linear_residual",
    )(a, w, res)


def _ffn_kernel(x_ref, g_ref, w1_ref, w3_ref, w2_ref, o_ref, h_ref):
    @pl.when(pl.program_id(1) == 0)
    def _():
        x = x_ref[...]
        h_ref[...] = _rms(x, g_ref[...]).astype(BF16)
        o_ref[...] = x

    h = h_ref[...]
    a = jnp.dot(h, w1_ref[...], preferred_element_type=F32)
    b = jnp.dot(h, w3_ref[...], preferred_element_type=F32)
    u = (a * jax.nn.sigmoid(a) * b).astype(BF16)
    o_ref[...] += jnp.dot(u, w2_ref[...], preferred_element_type=F32)


def ffn(x, g, w1, w3, w2):
    t, d = x.shape
    f = w1.shape[1]
    return pl.pallas_call(
        _ffn_kernel,
        out_shape=jax.ShapeDtypeStruct((t, d), F32),
        grid=(t // TM, f // TF),
        in_specs=[
            pl.BlockSpec((TM, d), lambda i, j: (i, 0)),
            pl.BlockSpec((1, d), lambda i, j: (0, 0)),
            pl.BlockSpec((d, TF), lambda i, j: (0, j)),
            pl.BlockSpec((d, TF), lambda i, j: (0, j)),
            pl.BlockSpec((TF, d), lambda i, j: (j, 0)),
        ],
        out_specs=pl.BlockSpec((TM, d), lambda i, j: (i, 0)),
        scratch_shapes=[pltpu.VMEM((TM, d), BF16)],
        compiler_params=_params("parallel", "arbitrary"),
        name="ffn",
    )(x, g.reshape(1, d), w1, w3, w2)


def _ple_kernel(*refs, final):
    if final:
        x_ref, g_ref, wpg_ref, p_ref, wpe_ref, gf_ref, o_ref = refs
    else:
        x_ref, g_ref, wpg_ref, p_ref, wpe_ref, o_ref = refs
    x = x_ref[...]
    h = _rms(x, g_ref[...]).astype(BF16)
    gate = jax.nn.sigmoid(jnp.dot(h, wpg_ref[...], preferred_element_type=F32))
    e = jnp.dot(p_ref[...].astype(BF16), wpe_ref[...], preferred_element_type=F32)
    y = x + e * gate
    if final:
        y = _rms(y, gf_ref[...])
    o_ref[...] = y


def ple_gate(x, g, w_pg, p, w_pe, g_final):
    t, d = x.shape
    pd = p.shape[1]
    final = g_final is not None
    in_specs = [
        pl.BlockSpec((TM, d), lambda i: (i, 0)),
        pl.BlockSpec((1, d), lambda i: (0, 0)),
        _resident((d, d), lambda i: (0, 0)),
        pl.BlockSpec((TM, pd), lambda i: (i, 0)),
        _resident((pd, d), lambda i: (0, 0)),
    ]
    args = [x, g.reshape(1, d), w_pg, p, w_pe]
    if final:
        in_specs.append(pl.BlockSpec((1, d), lambda i: (0, 0)))
        args.append(g_final.reshape(1, d))
    return pl.pallas_call(
        functools.partial(_ple_kernel, final=final),
        out_shape=jax.ShapeDtypeStruct((t, d), F32),
        grid=(t // TM,),
        in_specs=in_specs,
        out_specs=pl.BlockSpec((TM, d), lambda i: (i, 0)),
        compiler_params=_params("parallel"),
        name="ple_gate",
    )(*args)


def _swa_kernel(sink_ref, q_ref, kp_ref, kc_ref, vp_ref, vc_ref, tab_ref, o_ref):
    n = pl.program_id(1)
    tq = SWA_TQ
    row = lax.broadcasted_iota(I32, (tq, 2 * tq), 0)
    col = lax.broadcasted_iota(I32, (tq, 2 * tq), 1)
    dist = row + tq - col
    valid = (dist >= 0) & (dist < WINDOW) & ((col >= tq) | (n > 0))
    for g in range(N_KV):
        lo = g * HEAD_DIM
        kg = jnp.concatenate([kp_ref[:, lo:lo + HEAD_DIM], kc_ref[:, lo:lo + HEAD_DIM]], axis=0)
        vg = jnp.concatenate([vp_ref[:, lo:lo + HEAD_DIM], vc_ref[:, lo:lo + HEAD_DIM]], axis=0)
        qg = jnp.concatenate(
            [q_ref[:, (g * GROUP + r) * HEAD_DIM:(g * GROUP + r + 1) * HEAD_DIM] for r in range(GROUP)], axis=0)
        s_all = lax.dot_general(qg, kg, (((1,), (1,)), ((), ())), preferred_element_type=F32)
        probs = []
        for r in range(GROUP):
            h = g * GROUP + r
            s = s_all[r * tq:(r + 1) * tq] + tab_ref[h]
            s = jnp.where(valid, s, NEG)
            sink = sink_ref[h]
            m = jnp.maximum(jnp.max(s, axis=1, keepdims=True), sink)
            e = jnp.exp(s - m)
            denom = jnp.sum(e, axis=1, keepdims=True) + jnp.exp(sink - m)
            probs.append((e / denom).astype(BF16))
        pg = jnp.concatenate(probs, axis=0)
        og = jnp.dot(pg, vg, preferred_element_type=F32)
        for r in range(GROUP):
            h = g * GROUP + r
            o_ref[:, h * HEAD_DIM:(h + 1) * HEAD_DIM] = og[r * tq:(r + 1) * tq].astype(o_ref.dtype)


def swa_attention(q, kv, sinks, tab, batch):
    t, a_q = q.shape
    a_kv = N_KV * HEAD_DIM
    tq = SWA_TQ
    nb = t // batch // tq

    def prev_k(b, n):
        return (jnp.maximum(b * nb + n - 1, 0), 0)

    def prev_v(b, n):
        return (jnp.maximum(b * nb + n - 1, 0), 1)

    return pl.pallas_call(
        _swa_kernel,
        out_shape=jax.ShapeDtypeStruct((t, a_q), BF16),
        grid=(batch, nb),
        in_specs=[
            pl.BlockSpec(memory_space=pltpu.SMEM),
            pl.BlockSpec((tq, a_q), lambda b, n: (b * nb + n, 0)),
            pl.BlockSpec((tq, a_kv), prev_k),
            pl.BlockSpec((tq, a_kv), lambda b, n: (b * nb + n, 0)),
            pl.BlockSpec((tq, a_kv), prev_v),
            pl.BlockSpec((tq, a_kv), lambda b, n: (b * nb + n, 1)),
            _resident((N_HEADS, tq, 2 * tq), lambda b, n: (0, 0, 0)),
        ],
        out_specs=pl.BlockSpec((tq, a_q), lambda b, n: (b * nb + n, 0)),
        compiler_params=_params("parallel", "parallel"),
        name="swa_attention",
    )(sinks, q, kv, kv, kv, kv, tab)


def _key_to_f32(key):
    bits = jnp.where(key < 0, key ^ jnp.int32(0x7FFFFFFF), key)
    return pltpu.bitcast(bits, F32)


def _dsa_kernel(q_ref, k_ref, v_ref, qi_ref, ki_ref, wi_ref, tab_ref, o_ref,
                sc_ref, wb_ref, thr_ref, m_ref, l_ref, acc_ref, *, topk):
    t = DSA_T
    j = pl.program_id(1)
    nl = t // LANES
    rows = GROUP * t

    for h in range(IDX_HEADS):
        wb_ref[h] = jnp.broadcast_to(wi_ref[:, h:h + 1], (t, LANES))

    def score_tile(c, carry):
        kt = ki_ref[pl.ds(pl.multiple_of(c * t, t), t), :]
        acc = jnp.zeros((t, t), F32)
        for h in range(IDX_HEADS):
            s = lax.dot_general(qi_ref[h * t:(h + 1) * t, :], kt, (((1,), (1,)), ((), ())),
                                preferred_element_type=F32)
            w = jnp.concatenate([wb_ref[h]] * nl, axis=1)
            acc = acc + w * jnp.maximum(s, 0.0)
        sc_ref[c] = acc
        return carry

    lax.fori_loop(0, j + 1, score_tile, 0)
    row = lax.broadcasted_iota(I32, (t, t), 0)
    col = lax.broadcasted_iota(I32, (t, t), 1)
    causal = col <= row
    sc_ref[j] = jnp.where(causal, sc_ref[j], NEG)

    def bisect(it, prefix):
        cand_key = prefix + jnp.left_shift(jnp.int32(1), 31 - it)
        cand = _key_to_f32(cand_key)

        def count_tile(c, cnt):
            tile = sc_ref[c]
            for u in range(nl):
                cnt = cnt + jnp.where(tile[:, u * LANES:(u + 1) * LANES] >= cand, 1.0, 0.0)
            return cnt

        cnt = lax.fori_loop(0, j + 1, count_tile, jnp.zeros((t, LANES), F32))
        total = jnp.sum(cnt, axis=1, keepdims=True)
        return jnp.where(total >= float(topk), cand_key, prefix)

    prefix = lax.fori_loop(0, 32, bisect, jnp.full((t, LANES), INT_MIN, I32))
    thr_ref[...] = jnp.where(prefix == INT_MIN, -jnp.inf, _key_to_f32(prefix))

    zero_blk = jnp.zeros((LANES, LANES), F32)

    def bias_tile(h, kind):
        d_blk = tab_ref[h, 0]
        p_blk = tab_ref[h, 1]
        if kind == "diag":
            top = jnp.concatenate([d_blk, zero_blk], axis=1)
            bot = jnp.concatenate([p_blk, d_blk], axis=1)
        else:
            top = jnp.concatenate([zero_blk, p_blk], axis=1)
            bot = jnp.concatenate([zero_blk, zero_blk], axis=1)
        return jnp.concatenate([top, bot], axis=0)

    for g in range(N_KV):
        lo = g * HEAD_DIM
        qg = jnp.concatenate(
            [q_ref[:, (g * GROUP + r) * HEAD_DIM:(g * GROUP + r + 1) * HEAD_DIM] for r in range(GROUP)], axis=0)
        m_ref[...] = jnp.full((rows, LANES), NEG, F32)
        l_ref[...] = jnp.zeros((rows, LANES), F32)
        acc_ref[...] = jnp.zeros((rows, HEAD_DIM), F32)

        def attend(c, kind):
            start = pl.multiple_of(c * t, t)
            kt = k_ref[pl.ds(start, t), lo:lo + HEAD_DIM]
            vt = v_ref[pl.ds(start, t), lo:lo + HEAD_DIM]
            s = lax.dot_general(qg, kt, (((1,), (1,)), ((), ())), preferred_element_type=F32)
            thr = jnp.concatenate([thr_ref[...]] * nl, axis=1)
            sel = sc_ref[c] >= thr
            if kind == "diag":
                sel = sel & causal
            parts = []
            for r in range(GROUP):
                sr = s[r * t:(r + 1) * t]
                if kind != "far":
                    sr = sr + bias_tile(g * GROUP + r, kind)
                parts.append(jnp.where(sel, sr, NEG))
            s = jnp.concatenate(parts, axis=0)
            m_prev = m_ref[...]
            m_next = jnp.maximum(m_prev, jnp.max(s, axis=1, keepdims=True))
            alpha = jnp.exp(m_prev - m_next)
            p = jnp.exp(s - jnp.concatenate([m_next] * nl, axis=1))
            l_ref[...] = alpha * l_ref[...] + jnp.sum(p, axis=1, keepdims=True)
            acc_ref[...] = alpha * acc_ref[...] + jnp.dot(p.astype(BF16), vt, preferred_element_type=F32)
            m_ref[...] = m_next

        def far_step(c, carry):
            attend(c, "far")
            return carry

        lax.fori_loop(0, jnp.maximum(j - 1, 0), far_step, 0)

        @pl.when(j >= 1)
        def _():
            attend(j - 1, "prev")

        attend(j, "diag")
        out = acc_ref[...] / l_ref[...]
        for r in range(GROUP):
            h = g * GROUP + r
            o_ref[:, h * HEAD_DIM:(h + 1) * HEAD_DIM] = out[r * t:(r + 1) * t].astype(o_ref.dtype)


def dsa_attention(proj, qi, ki, wi, tab, batch, topk):
    tt = proj.shape[0]
    s = tt // batch
    t = DSA_T
    nq = s // t
    a_q = N_HEADS * HEAD_DIM
    a_kv = N_KV * HEAD_DIM
    k_blk = a_q // a_kv
    rows = GROUP * t
    return pl.pallas_call(
        functools.partial(_dsa_kernel, topk=topk),
        out_shape=jax.ShapeDtypeStruct((tt, a_q), BF16),
        grid=(batch, nq),
        in_specs=[
            pl.BlockSpec((t, a_q), lambda b, j: (b * nq + j, 0)),
            _resident((s, a_kv), lambda b, j: (b, k_blk)),
            _resident((s, a_kv), lambda b, j: (b, k_blk + 1)),
            pl.BlockSpec((IDX_HEADS * t, IDX_DIM), lambda b, j: (b * nq + j, 0)),
            _resident((s, IDX_DIM), lambda b, j: (b, 0)),
            pl.BlockSpec((t, IDX_HEADS), lambda b, j: (b * nq + j, 0)),
            _resident((N_HEADS, 2, LANES, LANES), lambda b, j: (0, 0, 0, 0)),
        ],
        out_specs=pl.BlockSpec((t, a_q), lambda b, j: (b * nq + j, 0)),
        scratch_shapes=[
            pltpu.VMEM((s // t, t, t), F32),
            pltpu.VMEM((IDX_HEADS, t, LANES), F32),
            pltpu.VMEM((t, LANES), F32),
            pltpu.VMEM((rows, LANES), F32),
            pltpu.VMEM((rows, LANES), F32),
            pltpu.VMEM((rows, HEAD_DIM), F32),
        ],
        compiler_params=_params("parallel", "arbitrary"),
        name="dsa_attention",
    )(proj, proj, proj, qi, ki, wi, tab)


def kernel(x, p, w_in_a, w_out_a, w_q_b, w_out_b, sinks, g_kv, w_kv, rel_bias,
           g_attn, g_ffn, w1, w3, w2, g_pe, w_pe, w_pg, g_final):
    batch, seq, d = x.shape
    depth = g_attn.shape[0]
    n_a = w_in_a.shape[0]
    tt = batch * seq
    a_q = N_HEADS * HEAD_DIM
    a_kv = N_KV * HEAD_DIM
    a_qi = IDX_HEADS * IDX_DIM
    n_main = a_q + 2 * a_kv + a_qi
    topk = min(TOPK_MAX, seq // 4)
    assert DSA_T == 2 * LANES and seq % DSA_T == 0 and topk <= DSA_T and tt % TM == 0

    attn_scale = HEAD_DIM ** -0.5
    scale_main = jnp.concatenate([
        jnp.full((a_q,), attn_scale, F32), jnp.ones((2 * a_kv,), F32), jnp.full((a_qi,), IDX_DIM ** -0.5, F32)])
    scale_tail = jnp.concatenate([
        jnp.ones((IDX_DIM,), F32), jnp.full((IDX_HEADS,), IDX_HEADS ** -0.5, F32),
        jnp.zeros((LANES - IDX_DIM - IDX_HEADS,), F32)])
    scale_q = jnp.full((a_q,), attn_scale, F32)

    r = np.arange(LANES)
    dist_diag = r[:, None] - r[None, :]
    dist_prev = dist_diag + LANES
    far_bucket = N_BUCKETS - 1
    assert (_t5_bucket_np(np.arange(LANES, 4 * LANES)) == far_bucket).all()
    rb = rel_bias.astype(F32)
    rb_shift = rb - rb[far_bucket][None, :]
    dsa_tab = jnp.stack([rb_shift[_t5_bucket_np(dist_diag)], rb_shift[_t5_bucket_np(dist_prev)]], axis=0)
    dsa_tab = dsa_tab.transpose(3, 0, 1, 2)
    swa_dist = r[:, None] + SWA_TQ - np.arange(2 * SWA_TQ)[None, :]
    swa_tab = rb[_t5_bucket_np(swa_dist)].transpose(2, 0, 1)

    x2 = x.reshape(tt, d)
    p2 = p.reshape(depth, tt, p.shape[-1])
    kv = None
    nq = seq // DSA_T
    for i in range(depth):
        if i < n_a:
            w_main = w_in_a[i, :, :n_main].astype(BF16)
            w_tail = jnp.pad(w_in_a[i, :, n_main:], ((0, 0), (0, LANES - IDX_DIM - IDX_HEADS))).astype(BF16)
            proj = norm_linear(x2, g_attn[i], w_main, scale_main, BF16, tn=1024)
            tail = norm_linear(x2, g_attn[i], w_tail, scale_tail, F32, tn=LANES)
            qi = proj[:, a_q + 2 * a_kv:].reshape(batch, nq, DSA_T, IDX_HEADS, IDX_DIM)
            qi = qi.transpose(0, 1, 3, 2, 4).reshape(tt * IDX_HEADS, IDX_DIM)
            ki = tail[:, :IDX_DIM].astype(BF16)
            wi = tail[:, IDX_DIM:IDX_DIM + IDX_HEADS]
            attn = dsa_attention(proj, qi, ki, wi, dsa_tab, batch, topk)
            x2 = linear_residual(attn, w_out_a[i].astype(BF16), x2)
        else:
            jb = i - n_a
            q = norm_linear(x2, g_attn[i], w_q_b[jb].astype(BF16), scale_q, BF16, tn=1024)
            attn = swa_attention(q, kv, sinks[jb], swa_tab, batch)
            x2 = linear_residual(attn, w_out_b[jb].astype(BF16), x2)
        x2 = ffn(x2, g_ffn[i], w1[i].astype(BF16), w3[i].astype(BF16), w2[i].astype(BF16))
        x2 = ple_gate(x2, g_pe[i], w_pg[i].astype(BF16), p2[i], w_pe[i].astype(BF16),
                      g_final if i == depth - 1 else None)
        if i == n_a - 1:
            kv = norm_linear(x2, g_kv, w_kv.astype(BF16), None, BF16, tn=1024)
    return x2.reshape(batch, seq, d)
```

```python
import functools
import math

import numpy as np
import jax
import jax.numpy as jnp
from jax import lax
from jax.experimental import pallas as pl
from jax.experimental.pallas import tpu as pltpu

F32 = jnp.float32
BF16 = jnp.bfloat16
I32 = jnp.int32

N_HEADS = 16
HEAD_DIM = 128
N_KV = 4
GROUP = N_HEADS // N_KV
IDX_HEADS = 16
IDX_DIM = 64
TOPK_MAX = 256
WINDOW = 128
N_BUCKETS = 32
MAX_DIST = 128
EPS = 1e-6
NEG = -1e30
INT_MIN = -(2 ** 31)
F32_MAX = float(np.finfo(np.float32).max)

V7X_VMEM_BYTES = 64 * 1024 * 1024
VMEM_LIMIT = V7X_VMEM_BYTES - 8 * 1024 * 1024
LANES = 128
SUBLANES = 8

TM = 512
TF = 512
DSA_T = 256
SWA_TQ = 128


def _params(*sem):
    return pltpu.CompilerParams(dimension_semantics=sem, vmem_limit_bytes=VMEM_LIMIT)


def _resident(shape, index_map):
    return pl.BlockSpec(shape, index_map, pipeline_mode=pl.Buffered(1))


def _rms(x, g):
    ms = jnp.mean(x * x, axis=-1, keepdims=True)
    return x * lax.rsqrt(ms + EPS) * g


def _t5_bucket_np(d):
    d = np.maximum(d, 0)
    max_exact = N_BUCKETS // 2
    nf = np.maximum(d, 1).astype(np.float32)
    large = max_exact + (np.log(nf / np.float32(max_exact)) / np.float32(math.log(MAX_DIST / max_exact))
                         * np.float32(N_BUCKETS - max_exact)).astype(np.int32)
    large = np.minimum(large, N_BUCKETS - 1)
    return np.where(d < max_exact, d, large)


def _norm_linear_kernel(*refs, scaled):
    if scaled:
        x_ref, g_ref, w_ref, s_ref, o_ref, h_ref = refs
    else:
        x_ref, g_ref, w_ref, o_ref, h_ref = refs

    @pl.when(pl.program_id(1) == 0)
    def _():
        h_ref[...] = _rms(x_ref[...], g_ref[...]).astype(BF16)

    y = jnp.dot(h_ref[...], w_ref[...], preferred_element_type=F32)
    if scaled:
        y = y * s_ref[...]
    o_ref[...] = y.astype(o_ref.dtype)


def norm_linear(x, g, w, col_scale, out_dtype, tn):
    t, d = x.shape
    n = w.shape[1]
    scaled = col_scale is not None
    in_specs = [
        pl.BlockSpec((TM, d), lambda i, j: (i, 0)),
        pl.BlockSpec((1, d), lambda i, j: (0, 0)),
        pl.BlockSpec((d, tn), lambda i, j: (0, j)),
    ]
    args = [x, g.reshape(1, d), w]
    if scaled:
        in_specs.append(pl.BlockSpec((1, tn), lambda i, j: (0, j)))
        args.append(col_scale.reshape(1, n))
    return pl.pallas_call(
        functools.partial(_norm_linear_kernel, scaled=scaled),
        out_shape=jax.ShapeDtypeStruct((t, n), out_dtype),
        grid=(t // TM, n // tn),
        in_specs=in_specs,
        out_specs=pl.BlockSpec((TM, tn), lambda i, j: (i, j)),
        scratch_shapes=[pltpu.VMEM((TM, d), BF16)],
        compiler_params=_params("parallel", "arbitrary"),
        name="norm_linear",
    )(*args)


def _linear_residual_kernel(a_ref, w_ref, r_ref, o_ref):
    o_ref[...] = r_ref[...] + jnp.dot(a_ref[...], w_ref[...], preferred_element_type=F32)


def linear_residual(a, w, res):
    t, k = a.shape
    n = w.shape[1]
    return pl.pallas_call(
        _linear_residual_kernel,
        out_shape=jax.ShapeDtypeStruct((t, n), F32),
        grid=(t // TM,),
        in_specs=[
            pl.BlockSpec((TM, k), lambda i: (i, 0)),
            _resident((k, n), lambda i: (0, 0)),
            pl.BlockSpec((TM, n), lambda i: (i, 0)),
        ],
        out_specs=pl.BlockSpec((TM, n), lambda i: (i, 0)),
        compiler_params=_params("parallel"),
        name="linear_residual",
    )(a, w, res)


def _ffn_kernel(x_ref, g_ref, w1_ref, w3_ref, w2_ref, o_ref, h_ref):
    @pl.when(pl.program_id(1) == 0)
    def _():
        x = x_ref[...]
        h_ref[...] = _rms(x, g_ref[...]).astype(BF16)
        o_ref[...] = x

    h = h_ref[...]
    a = jnp.dot(h, w1_ref[...], preferred_element_type=F32)
    b = jnp.dot(h, w3_ref[...], preferred_element_type=F32)
    u = (a * jax.nn.sigmoid(a) * b).astype(BF16)
    o_ref[...] += jnp.dot(u, w2_ref[...], preferred_element_type=F32)


def ffn(x, g, w1, w3, w2):
    t, d = x.shape
    f = w1.shape[1]
    return pl.pallas_call(
        _ffn_kernel,
        out_shape=jax.ShapeDtypeStruct((t, d), F32),
        grid=(t // TM, f // TF),
        in_specs=[
            pl.BlockSpec((TM, d), lambda i, j: (i, 0)),
            pl.BlockSpec((1, d), lambda i, j: (0, 0)),
            pl.BlockSpec((d, TF), lambda i, j: (0, j)),
            pl.BlockSpec((d, TF), lambda i, j: (0, j)),
            pl.BlockSpec((TF, d), lambda i, j: (j, 0)),
        ],
        out_specs=pl.BlockSpec((TM, d), lambda i, j: (i, 0)),
        scratch_shapes=[pltpu.VMEM((TM, d), BF16)],
        compiler_params=_params("parallel", "arbitrary"),
        name="ffn",
    )(x, g.reshape(1, d), w1, w3, w2)


def _ple_kernel(*refs, final):
    if final:
        x_ref, g_ref, wpg_ref, p_ref, wpe_ref, gf_ref, o_ref = refs
    else:
        x_ref, g_ref, wpg_ref, p_ref, wpe_ref, o_ref = refs
    x = x_ref[...]
    h = _rms(x, g_ref[...]).astype(BF16)
    gate = jax.nn.sigmoid(jnp.dot(h, wpg_ref[...], preferred_element_type=F32))
    e = jnp.dot(p_ref[...].astype(BF16), wpe_ref[...], preferred_element_type=F32)
    y = x + e * gate
    if final:
        y = _rms(y, gf_ref[...])
    o_ref[...] = y


def ple_gate(x, g, w_pg, p, w_pe, g_final):
    t, d = x.shape
    pd = p.shape[1]
    final = g_final is not None
    in_specs = [
        pl.BlockSpec((TM, d), lambda i: (i, 0)),
        pl.BlockSpec((1, d), lambda i: (0, 0)),
        _resident((d, d), lambda i: (0, 0)),
        pl.BlockSpec((TM, pd), lambda i: (i, 0)),
        _resident((pd, d), lambda i: (0, 0)),
    ]
    args = [x, g.reshape(1, d), w_pg, p, w_pe]
    if final:
        in_specs.append(pl.BlockSpec((1, d), lambda i: (0, 0)))
        args.append(g_final.reshape(1, d))
    return pl.pallas_call(
        functools.partial(_ple_kernel, final=final),
        out_shape=jax.ShapeDtypeStruct((t, d), F32),
        grid=(t // TM,),
        in_specs=in_specs,
        out_specs=pl.BlockSpec((TM, d), lambda i: (i, 0)),
        compiler_params=_params("parallel"),
        name="ple_gate",
    )(*args)


def _swa_kernel(sink_ref, q_ref, kp_ref, kc_ref, vp_ref, vc_ref, tab_ref, o_ref):
    n = pl.program_id(1)
    tq = SWA_TQ
    row = lax.broadcasted_iota(I32, (tq, 2 * tq), 0)
    col = lax.broadcasted_iota(I32, (tq, 2 * tq), 1)
    dist = row + tq - col
    valid = (dist >= 0) & (dist < WINDOW) & ((col >= tq) | (n > 0))
    for g in range(N_KV):
        lo = g * HEAD_DIM
        kg = jnp.concatenate([kp_ref[:, lo:lo + HEAD_DIM], kc_ref[:, lo:lo + HEAD_DIM]], axis=0)
        vg = jnp.concatenate([vp_ref[:, lo:lo + HEAD_DIM], vc_ref[:, lo:lo + HEAD_DIM]], axis=0)
        qg = jnp.concatenate(
            [q_ref[:, (g * GROUP + r) * HEAD_DIM:(g * GROUP + r + 1) * HEAD_DIM] for r in range(GROUP)], axis=0)
        s_all = lax.dot_general(qg, kg, (((1,), (1,)), ((), ())), preferred_element_type=F32)
        probs = []
        for r in range(GROUP):
            h = g * GROUP + r
            s = s_all[r * tq:(r + 1) * tq] + tab_ref[h]
            s = jnp.where(valid, s, NEG)
            sink = sink_ref[h]
            m = jnp.maximum(jnp.max(s, axis=1, keepdims=True), sink)
            e = jnp.exp(s - m)
            denom = jnp.sum(e, axis=1, keepdims=True) + jnp.exp(sink - m)
            probs.append((e / denom).astype(BF16))
        pg = jnp.concatenate(probs, axis=0)
        og = jnp.dot(pg, vg, preferred_element_type=F32)
        for r in range(GROUP):
            h = g * GROUP + r
            o_ref[:, h * HEAD_DIM:(h + 1) * HEAD_DIM] = og[r * tq:(r + 1) * tq].astype(o_ref.dtype)


def swa_attention(q, kv, sinks, tab, batch):
    t, a_q = q.shape
    a_kv = N_KV * HEAD_DIM
    tq = SWA_TQ
    nb = t // batch // tq

    def prev_k(b, n):
        return (jnp.maximum(b * nb + n - 1, 0), 0)

    def prev_v(b, n):
        return (jnp.maximum(b * nb + n - 1, 0), 1)

    return pl.pallas_call(
        _swa_kernel,
        out_shape=jax.ShapeDtypeStruct((t, a_q), BF16),
        grid=(batch, nb),
        in_specs=[
            pl.BlockSpec(memory_space=pltpu.SMEM),
            pl.BlockSpec((tq, a_q), lambda b, n: (b * nb + n, 0)),
            pl.BlockSpec((tq, a_kv), prev_k),
            pl.BlockSpec((tq, a_kv), lambda b, n: (b * nb + n, 0)),
            pl.BlockSpec((tq, a_kv), prev_v),
            pl.BlockSpec((tq, a_kv), lambda b, n: (b * nb + n, 1)),
            _resident((N_HEADS, tq, 2 * tq), lambda b, n: (0, 0, 0)),
        ],
        out_specs=pl.BlockSpec((tq, a_q), lambda b, n: (b * nb + n, 0)),
        compiler_params=_params("parallel", "parallel"),
        name="swa_attention",
    )(sinks, q, kv, kv, kv, kv, tab)


def _key_to_f32(key):
    bits = jnp.where(key < 0, key ^ jnp.int32(0x7FFFFFFF), key)
    return pltpu.bitcast(bits, F32)


def _dsa_kernel(q_ref, k_ref, v_ref, qit_ref, ki_ref, wit_ref, tab_ref, o_ref,
                sc_ref, qg_ref, m_ref, l_ref, acc_ref, *, topk):
    t = DSA_T
    j = pl.program_id(1)
    rows = GROUP * t

    def score_tile(c, carry):
        kt = ki_ref[pl.ds(pl.multiple_of(c * t, t), t), :]
        acc = jnp.zeros((t, t), F32)
        for h in range(IDX_HEADS):
            s = jnp.dot(kt, qit_ref[h * IDX_DIM:(h + 1) * IDX_DIM, :], preferred_element_type=F32)
            acc = acc + wit_ref[0, h:h + 1, :] * jnp.maximum(s, 0.0)
        sc_ref[c] = acc
        return carry

    lax.fori_loop(0, j + 1, score_tile, 0)
    key_i = lax.broadcasted_iota(I32, (t, t), 0)
    qry_i = lax.broadcasted_iota(I32, (t, t), 1)
    causal = key_i <= qry_i
    sc_ref[j] = jnp.where(causal, sc_ref[j], NEG)

    @pl.when((j + 1) % 2 == 1)
    def _():
        sc_ref[j + 1] = jnp.full((t, t), NEG, F32)

    def bisect(it, prefix):
        cand_key = prefix + jnp.left_shift(jnp.int32(1), 31 - it)
        cand = _key_to_f32(cand_key)[None]

        def count_pair(pi, cnts):
            cnts = list(cnts)
            for u in range(2):
                for k in range(t // SUBLANES):
                    slab = sc_ref[2 * pi + u, k * SUBLANES:(k + 1) * SUBLANES, :]
                    a = k % len(cnts)
                    cnts[a] = jnp.where(slab >= cand[0], cnts[a] + 1.0, cnts[a])
            return tuple(cnts)

        cnts = lax.fori_loop(0, (j + 2) // 2, count_pair, tuple(jnp.zeros((SUBLANES, t), F32) for _ in range(4)))
        cnt = (cnts[0] + cnts[1]) + (cnts[2] + cnts[3])
        total = jnp.sum(cnt, axis=0, keepdims=True)
        return jnp.where(total >= float(topk), cand_key, prefix)

    prefix = lax.fori_loop(0, 32, bisect, jnp.full((1, t), INT_MIN, I32))
    thr = jnp.where(prefix == INT_MIN, -jnp.inf, _key_to_f32(prefix))

    for g in range(N_KV):
        for r in range(GROUP):
            h = g * GROUP + r
            qg_ref[g, r * t:(r + 1) * t, :] = q_ref[:, h * HEAD_DIM:(h + 1) * HEAD_DIM]
    m_ref[...] = jnp.full(m_ref.shape, NEG, F32)
    l_ref[...] = jnp.zeros(l_ref.shape, F32)
    acc_ref[...] = jnp.zeros(acc_ref.shape, F32)
    zero_blk = jnp.zeros((LANES, LANES), F32)

    def bias_rows(h, kind, q0):
        d_blk = tab_ref[h, 0]
        p_blk = tab_ref[h, 1]
        if kind == "diag":
            return jnp.concatenate([d_blk, zero_blk] if q0 == 0 else [p_blk, d_blk], axis=1)
        return jnp.concatenate([zero_blk, p_blk], axis=1) if q0 == 0 else None

    def attend(c, ntile, kind):
        width = ntile * t
        rc = (LANES * t) // width
        start = pl.multiple_of(c * t, t)
        caps = []
        for u in range(ntile):
            cap_kq = jnp.where(sc_ref[c + u] >= thr, F32_MAX, NEG)
            if kind == "diag":
                cap_kq = jnp.where(causal, cap_kq, NEG)
            caps.append(cap_kq.T)
        cap = jnp.concatenate(caps, axis=1) if ntile > 1 else caps[0]
        for g in range(N_KV):
            lo = g * HEAD_DIM
            kt = k_ref[pl.ds(start, width), lo:lo + HEAD_DIM]
            vt = v_ref[pl.ds(start, width), lo:lo + HEAD_DIM]
            s_all = lax.dot_general(qg_ref[g], kt, (((1,), (1,)), ((), ())), preferred_element_type=F32)
            probs, alphas = [], []
            for ch in range(rows // rc):
                rs = slice(ch * rc, (ch + 1) * rc)
                r, q0 = divmod(ch * rc, t)
                s = s_all[rs]
                if kind != "far":
                    b = bias_rows(g * GROUP + r, kind, q0)
                    if b is not None:
                        s = s + b
                s = jnp.minimum(s, cap[q0:q0 + rc])
                m_prev = m_ref[g, rs]
                m_next = jnp.maximum(m_prev, jnp.max(s, axis=1, keepdims=True))
                alpha = jnp.exp2(m_prev - m_next)
                p = jnp.exp2(s - jnp.concatenate([m_next] * (width // LANES), axis=1))
                psum = p[:, :LANES]
                for w in range(1, width // LANES):
                    psum = psum + p[:, w * LANES:(w + 1) * LANES]
                l_ref[g, rs] = alpha * l_ref[g, rs] + psum
                m_ref[g, rs] = m_next
                probs.append(p.astype(BF16))
                alphas.append(alpha)
            pv = jnp.dot(jnp.concatenate(probs, axis=0), vt, preferred_element_type=F32)
            acc_ref[g] = jnp.concatenate(alphas, axis=0) * acc_ref[g] + pv

    n_far = jnp.maximum(j - 1, 0)

    def far_pair(c2, carry):
        attend(2 * c2, 2, "far")
        return carry

    lax.fori_loop(0, n_far // 2, far_pair, 0)

    @pl.when(n_far % 2 == 1)
    def _():
        attend(n_far - 1, 1, "far")

    @pl.when(j >= 1)
    def _():
        attend(j - 1, 1, "prev")

    attend(j, 1, "diag")
    for g in range(N_KV):
        out = acc_ref[g] / jnp.sum(l_ref[g], axis=1, keepdims=True)
        for r in range(GROUP):
            h = g * GROUP + r
            o_ref[:, h * HEAD_DIM:(h + 1) * HEAD_DIM] = out[r * t:(r + 1) * t].astype(o_ref.dtype)


def dsa_attention(proj, qit, ki, wit, tab, batch, topk):
    tt = proj.shape[0]
    s = tt // batch
    t = DSA_T
    nq = s // t
    a_q = N_HEADS * HEAD_DIM
    a_kv = N_KV * HEAD_DIM
    a_qi = IDX_HEADS * IDX_DIM
    k_blk = a_q // a_kv
    rows = GROUP * t
    return pl.pallas_call(
        functools.partial(_dsa_kernel, topk=topk),
        out_shape=jax.ShapeDtypeStruct((tt, a_q), BF16),
        grid=(batch, nq),
        in_specs=[
            pl.BlockSpec((t, a_q), lambda b, j: (b * nq + j, 0)),
            _resident((s, a_kv), lambda b, j: (b, k_blk)),
            _resident((s, a_kv), lambda b, j: (b, k_blk + 1)),
            pl.BlockSpec((a_qi, t), lambda b, j: (b * nq + j, 0)),
            _resident((s, IDX_DIM), lambda b, j: (b, 0)),
            pl.BlockSpec((1, IDX_HEADS, t), lambda b, j: (b * nq + j, 0, 0)),
            _resident((N_HEADS, 2, LANES, LANES), lambda b, j: (0, 0, 0, 0)),
        ],
        out_specs=pl.BlockSpec((t, a_q), lambda b, j: (b * nq + j, 0)),
        scratch_shapes=[
            pltpu.VMEM((nq, t, t), F32),
            pltpu.VMEM((N_KV, rows, HEAD_DIM), BF16),
            pltpu.VMEM((N_KV, rows, LANES), F32),
            pltpu.VMEM((N_KV, rows, LANES), F32),
            pltpu.VMEM((N_KV, rows, HEAD_DIM), F32),
        ],
        compiler_params=_params("parallel", "arbitrary"),
        name="dsa_attention",
    )(proj, proj, proj, qit, ki, wit, tab)


def kernel(x, p, w_in_a, w_out_a, w_q_b, w_out_b, sinks, g_kv, w_kv, rel_bias,
           g_attn, g_ffn, w1, w3, w2, g_pe, w_pe, w_pg, g_final):
    batch, seq, d = x.shape
    depth = g_attn.shape[0]
    n_a = w_in_a.shape[0]
    tt = batch * seq
    a_q = N_HEADS * HEAD_DIM
    a_kv = N_KV * HEAD_DIM
    a_qi = IDX_HEADS * IDX_DIM
    n_main = a_q + 2 * a_kv + a_qi
    topk = min(TOPK_MAX, seq // 4)
    assert DSA_T == 2 * LANES and seq % (2 * DSA_T) == 0 and topk <= DSA_T and tt % TM == 0

    attn_scale = HEAD_DIM ** -0.5
    log2e = math.log2(math.e)
    scale_main = jnp.concatenate([
        jnp.full((a_q,), attn_scale * log2e, F32), jnp.ones((2 * a_kv,), F32),
        jnp.full((a_qi,), IDX_DIM ** -0.5, F32)])
    scale_tail = jnp.concatenate([
        jnp.ones((IDX_DIM,), F32), jnp.full((IDX_HEADS,), IDX_HEADS ** -0.5, F32),
        jnp.zeros((LANES - IDX_DIM - IDX_HEADS,), F32)])
    scale_q = jnp.full((a_q,), attn_scale, F32)

    r = np.arange(LANES)
    dist_diag = r[:, None] - r[None, :]
    dist_prev = dist_diag + LANES
    far_bucket = N_BUCKETS - 1
    assert (_t5_bucket_np(np.arange(LANES, 4 * LANES)) == far_bucket).all()
    rb = rel_bias.astype(F32)
    rb_shift = (rb - rb[far_bucket][None, :]) * log2e
    dsa_tab = jnp.stack([rb_shift[_t5_bucket_np(dist_diag)], rb_shift[_t5_bucket_np(dist_prev)]], axis=0)
    dsa_tab = dsa_tab.transpose(3, 0, 1, 2)
    swa_dist = r[:, None] + SWA_TQ - np.arange(2 * SWA_TQ)[None, :]
    swa_tab = rb[_t5_bucket_np(swa_dist)].transpose(2, 0, 1)

    x2 = x.reshape(tt, d)
    p2 = p.reshape(depth, tt, p.shape[-1])
    kv = None
    t = DSA_T
    nq = seq // t
    for i in range(depth):
        if i < n_a:
            w_main = w_in_a[i, :, :n_main].astype(BF16)
            w_tail = jnp.pad(w_in_a[i, :, n_main:], ((0, 0), (0, LANES - IDX_DIM - IDX_HEADS))).astype(BF16)
            proj = norm_linear(x2, g_attn[i], w_main, scale_main, BF16, tn=1024)
            tail = norm_linear(x2, g_attn[i], w_tail, scale_tail, F32, tn=LANES)
            qit = proj[:, a_q + 2 * a_kv:].reshape(batch, nq, t, a_qi)
            qit = qit.transpose(0, 1, 3, 2).reshape(batch * nq * a_qi, t)
            ki = tail[:, :IDX_DIM].astype(BF16)
            wit = tail[:, IDX_DIM:IDX_DIM + IDX_HEADS].reshape(batch * nq, t, IDX_HEADS).transpose(0, 2, 1)
            attn = dsa_attention(proj, qit, ki, wit, dsa_tab, batch, topk)
            x2 = linear_residual(attn, w_out_a[i].astype(BF16), x2)
        else:
            jb = i - n_a
            q = norm_linear(x2, g_attn[i], w_q_b[jb].astype(BF16), scale_q, BF16, tn=1024)
            attn = swa_attention(q, kv, sinks[jb], swa_tab, batch)
            x2 = linear_residual(attn, w_out_b[jb].astype(BF16), x2)
        x2 = ffn(x2, g_ffn[i], w1[i].astype(BF16), w3[i].astype(BF16), w2[i].astype(BF16))
        x2 = ple_gate(x2, g_pe[i], w_pg[i].astype(BF16), p2[i], w_pe[i].astype(BF16),
                      g_final if i == depth - 1 else None)
        if i == n_a - 1:
            kv = norm_linear(x2, g_kv, w_kv.astype(BF16), None, BF16, tn=1024)
    return x2.reshape(batch, seq, d)
```

```python
import functools
import math

import numpy as np
import jax
import jax.numpy as jnp
from jax import lax
from jax.experimental import pallas as pl
from jax.experimental.pallas import tpu as pltpu

F32 = jnp.float32
BF16 = jnp.bfloat16
I32 = jnp.int32

N_HEADS = 16
HEAD_DIM = 128
N_KV = 4
GROUP = N_HEADS // N_KV
IDX_HEADS = 16
IDX_DIM = 64
TOPK_MAX = 256
WINDOW = 128
N_BUCKETS = 32
MAX_DIST = 128
EPS = 1e-6
NEG = -1e30
INT_MIN = -(2 ** 31)
F32_MAX = float(np.finfo(np.float32).max)

V7X_VMEM_BYTES = 64 * 1024 * 1024
VMEM_LIMIT = V7X_VMEM_BYTES - 8 * 1024 * 1024
LANES = 128
SUBLANES = 8

TM = 512
TF = 512
DSA_T = 256
SWA_TQ = 128


def _params(*sem):
    return pltpu.CompilerParams(dimension_semantics=sem, vmem_limit_bytes=VMEM_LIMIT)


def _resident(shape, index_map):
    return pl.BlockSpec(shape, index_map, pipeline_mode=pl.Buffered(1))


def _rms(x, g):
    ms = jnp.mean(x * x, axis=-1, keepdims=True)
    return x * lax.rsqrt(ms + EPS) * g


def _t5_bucket_np(d):
    d = np.maximum(d, 0)
    max_exact = N_BUCKETS // 2
    nf = np.maximum(d, 1).astype(np.float32)
    large = max_exact + (np.log(nf / np.float32(max_exact)) / np.float32(math.log(MAX_DIST / max_exact))
                         * np.float32(N_BUCKETS - max_exact)).astype(np.int32)
    large = np.minimum(large, N_BUCKETS - 1)
    return np.where(d < max_exact, d, large)


def _norm_linear_kernel(*refs, scaled):
    if scaled:
        x_ref, g_ref, w_ref, s_ref, o_ref, h_ref = refs
    else:
        x_ref, g_ref, w_ref, o_ref, h_ref = refs

    @pl.when(pl.program_id(1) == 0)
    def _():
        h_ref[...] = _rms(x_ref[...], g_ref[...]).astype(BF16)

    y = jnp.dot(h_ref[...], w_ref[...], preferred_element_type=F32)
    if scaled:
        y = y * s_ref[...]
    o_ref[...] = y.astype(o_ref.dtype)


def _weight_spec(layer, block, index_map, resident=False):
    mode = dict(pipeline_mode=pl.Buffered(1)) if resident else {}
    if layer is None:
        return pl.BlockSpec(block, index_map, **mode)
    return pl.BlockSpec((None,) + tuple(block), lambda *i: (layer,) + tuple(index_map(*i)), **mode)


def norm_linear(x, g, w, layer, col_scale, out_dtype, tn):
    t, d = x.shape
    n = w.shape[-1]
    scaled = col_scale is not None
    in_specs = [
        pl.BlockSpec((TM, d), lambda i, j: (i, 0)),
        pl.BlockSpec((1, d), lambda i, j: (0, 0)),
        _weight_spec(layer, (d, tn), lambda i, j: (0, j)),
    ]
    args = [x, g.reshape(1, d), w]
    if scaled:
        in_specs.append(pl.BlockSpec((1, tn), lambda i, j: (0, j)))
        args.append(col_scale.reshape(1, n))
    return pl.pallas_call(
        functools.partial(_norm_linear_kernel, scaled=scaled),
        out_shape=jax.ShapeDtypeStruct((t, n), out_dtype),
        grid=(t // TM, n // tn),
        in_specs=in_specs,
        out_specs=pl.BlockSpec((TM, tn), lambda i, j: (i, j)),
        scratch_shapes=[pltpu.VMEM((TM, d), BF16)],
        compiler_params=_params("parallel", "arbitrary"),
        name="norm_linear",
    )(*args)


def _norm_linear_t_kernel(x_ref, g_ref, wt_ref, qit_ref, wit_ref, *, n_q, q_scale, w_scale):
    h = _rms(x_ref[...], g_ref[...]).astype(BF16)
    yt = lax.dot_general(wt_ref[...], h, (((1,), (1,)), ((), ())), preferred_element_type=F32)
    qit_ref[...] = (yt[:n_q] * q_scale).astype(qit_ref.dtype)
    wit_ref[...] = yt[n_q:] * w_scale


def norm_linear_t(x, g, wt, layer, n_q, q_scale, w_scale):
    t, d = x.shape
    n = wt.shape[-2]
    return pl.pallas_call(
        functools.partial(_norm_linear_t_kernel, n_q=n_q, q_scale=q_scale, w_scale=w_scale),
        out_shape=(jax.ShapeDtypeStruct((n_q, t), BF16), jax.ShapeDtypeStruct((n - n_q, t), F32)),
        grid=(t // TM,),
        in_specs=[
            pl.BlockSpec((TM, d), lambda i: (i, 0)),
            pl.BlockSpec((1, d), lambda i: (0, 0)),
            _weight_spec(layer, (n, d), lambda i: (0, 0), resident=True),
        ],
        out_specs=(pl.BlockSpec((n_q, TM), lambda i: (0, i)), pl.BlockSpec((n - n_q, TM), lambda i: (0, i))),
        compiler_params=_params("parallel"),
        name="norm_linear_t",
    )(x, g.reshape(1, d), wt)


def _linear_residual_kernel(a_ref, w_ref, r_ref, o_ref):
    o_ref[...] = r_ref[...] + jnp.dot(a_ref[...], w_ref[...], preferred_element_type=F32)


def linear_residual(a, w, layer, res):
    t, k = a.shape
    n = w.shape[-1]
    return pl.pallas_call(
        _linear_residual_kernel,
        out_shape=jax.ShapeDtypeStruct((t, n), F32),
        grid=(t // TM,),
        in_specs=[
            pl.BlockSpec((TM, k), lambda i: (i, 0)),
            _weight_spec(layer, (k, n), lambda i: (0, 0), resident=True),
            pl.BlockSpec((TM, n), lambda i: (i, 0)),
        ],
        out_specs=pl.BlockSpec((TM, n), lambda i: (i, 0)),
        compiler_params=_params("parallel"),
        name="linear_residual",
    )(a, w, res)


def _ffn_kernel(x_ref, g_ref, w1_ref, w3_ref, w2_ref, o_ref, h_ref):
    @pl.when(pl.program_id(1) == 0)
    def _():
        x = x_ref[...]
        h_ref[...] = _rms(x, g_ref[...]).astype(BF16)
        o_ref[...] = x

    h = h_ref[...]
    a = jnp.dot(h, w1_ref[...], preferred_element_type=F32)
    b = jnp.dot(h, w3_ref[...], preferred_element_type=F32)
    u = (a * jax.nn.sigmoid(a) * b).astype(BF16)
    o_ref[...] += jnp.dot(u, w2_ref[...], preferred_element_type=F32)


def ffn(x, g, w1, w3, w2, layer):
    t, d = x.shape
    f = w1.shape[-1]
    return pl.pallas_call(
        _ffn_kernel,
        out_shape=jax.ShapeDtypeStruct((t, d), F32),
        grid=(t // TM, f // TF),
        in_specs=[
            pl.BlockSpec((TM, d), lambda i, j: (i, 0)),
            pl.BlockSpec((1, d), lambda i, j: (0, 0)),
            _weight_spec(layer, (d, TF), lambda i, j: (0, j)),
            _weight_spec(layer, (d, TF), lambda i, j: (0, j)),
            _weight_spec(layer, (TF, d), lambda i, j: (j, 0)),
        ],
        out_specs=pl.BlockSpec((TM, d), lambda i, j: (i, 0)),
        scratch_shapes=[pltpu.VMEM((TM, d), BF16)],
        compiler_params=_params("parallel", "arbitrary"),
        name="ffn",
    )(x, g.reshape(1, d), w1, w3, w2)


def _ple_kernel(*refs, final):
    if final:
        x_ref, g_ref, wpg_ref, p_ref, wpe_ref, gf_ref, o_ref = refs
    else:
        x_ref, g_ref, wpg_ref, p_ref, wpe_ref, o_ref = refs
    x = x_ref[...]
    h = _rms(x, g_ref[...]).astype(BF16)
    gate = jax.nn.sigmoid(jnp.dot(h, wpg_ref[...], preferred_element_type=F32))
    e = jnp.dot(p_ref[...].astype(BF16), wpe_ref[...], preferred_element_type=F32)
    y = x + e * gate
    if final:
        y = _rms(y, gf_ref[...])
    o_ref[...] = y


def ple_gate(x, g, w_pg, p, w_pe, layer, g_final):
    t, d = x.shape
    pd = p.shape[-1]
    final = g_final is not None
    in_specs = [
        pl.BlockSpec((TM, d), lambda i: (i, 0)),
        pl.BlockSpec((1, d), lambda i: (0, 0)),
        _weight_spec(layer, (d, d), lambda i: (0, 0), resident=True),
        _weight_spec(layer, (TM, pd), lambda i: (i, 0)),
        _weight_spec(layer, (pd, d), lambda i: (0, 0), resident=True),
    ]
    args = [x, g.reshape(1, d), w_pg, p, w_pe]
    if final:
        in_specs.append(pl.BlockSpec((1, d), lambda i: (0, 0)))
        args.append(g_final.reshape(1, d))
    return pl.pallas_call(
        functools.partial(_ple_kernel, final=final),
        out_shape=jax.ShapeDtypeStruct((t, d), F32),
        grid=(t // TM,),
        in_specs=in_specs,
        out_specs=pl.BlockSpec((TM, d), lambda i: (i, 0)),
        compiler_params=_params("parallel"),
        name="ple_gate",
    )(*args)


def _swa_kernel(sink_ref, q_ref, kp_ref, kc_ref, vp_ref, vc_ref, tab_ref, o_ref):
    n = pl.program_id(1)
    tq = SWA_TQ
    row = lax.broadcasted_iota(I32, (tq, 2 * tq), 0)
    col = lax.broadcasted_iota(I32, (tq, 2 * tq), 1)
    dist = row + tq - col
    valid = (dist >= 0) & (dist < WINDOW) & ((col >= tq) | (n > 0))
    for g in range(N_KV):
        lo = g * HEAD_DIM
        kg = jnp.concatenate([kp_ref[:, lo:lo + HEAD_DIM], kc_ref[:, lo:lo + HEAD_DIM]], axis=0)
        vg = jnp.concatenate([vp_ref[:, lo:lo + HEAD_DIM], vc_ref[:, lo:lo + HEAD_DIM]], axis=0)
        qg = jnp.concatenate(
            [q_ref[:, (g * GROUP + r) * HEAD_DIM:(g * GROUP + r + 1) * HEAD_DIM] for r in range(GROUP)], axis=0)
        s_all = lax.dot_general(qg, kg, (((1,), (1,)), ((), ())), preferred_element_type=F32)
        probs = []
        for r in range(GROUP):
            h = g * GROUP + r
            s = s_all[r * tq:(r + 1) * tq] + tab_ref[h]
            s = jnp.where(valid, s, NEG)
            sink = sink_ref[h]
            m = jnp.maximum(jnp.max(s, axis=1, keepdims=True), sink)
            e = jnp.exp(s - m)
            denom = jnp.sum(e, axis=1, keepdims=True) + jnp.exp(sink - m)
            probs.append((e / denom).astype(BF16))
        pg = jnp.concatenate(probs, axis=0)
        og = jnp.dot(pg, vg, preferred_element_type=F32)
        for r in range(GROUP):
            h = g * GROUP + r
            o_ref[:, h * HEAD_DIM:(h + 1) * HEAD_DIM] = og[r * tq:(r + 1) * tq].astype(o_ref.dtype)


def swa_attention(q, kv, sinks, tab, batch):
    t, a_q = q.shape
    a_kv = N_KV * HEAD_DIM
    tq = SWA_TQ
    nb = t // batch // tq

    def prev_k(b, n):
        return (jnp.maximum(b * nb + n - 1, 0), 0)

    def prev_v(b, n):
        return (jnp.maximum(b * nb + n - 1, 0), 1)

    return pl.pallas_call(
        _swa_kernel,
        out_shape=jax.ShapeDtypeStruct((t, a_q), BF16),
        grid=(batch, nb),
        in_specs=[
            pl.BlockSpec(memory_space=pltpu.SMEM),
            pl.BlockSpec((tq, a_q), lambda b, n: (b * nb + n, 0)),
            pl.BlockSpec((tq, a_kv), prev_k),
            pl.BlockSpec((tq, a_kv), lambda b, n: (b * nb + n, 0)),
            pl.BlockSpec((tq, a_kv), prev_v),
            pl.BlockSpec((tq, a_kv), lambda b, n: (b * nb + n, 1)),
            _resident((N_HEADS, tq, 2 * tq), lambda b, n: (0, 0, 0)),
        ],
        out_specs=pl.BlockSpec((tq, a_q), lambda b, n: (b * nb + n, 0)),
        compiler_params=_params("parallel", "parallel"),
        name="swa_attention",
    )(sinks, q, kv, kv, kv, kv, tab)


def _key_to_f32(key):
    bits = jnp.where(key < 0, key ^ jnp.int32(0x7FFFFFFF), key)
    return pltpu.bitcast(bits, F32)


def _f32_to_key(x):
    bits = pltpu.bitcast(x, I32)
    return jnp.where(bits < 0, bits ^ jnp.int32(0x7FFFFFFF), bits)


def _dsa_kernel(q_ref, k_ref, v_ref, qit_ref, ki_ref, wit_ref, tab_ref, o_ref,
                sc_ref, gmax_ref, qg_ref, m_ref, l_ref, acc_ref, *, topk, idx_steps):
    t = DSA_T
    j = pl.program_id(1)
    rows = GROUP * t

    def score_tile(c):
        kt = ki_ref[pl.ds(pl.multiple_of(c * t, t), t), :IDX_DIM]
        acc = jnp.zeros((t, t), F32)
        for h in range(IDX_HEADS):
            s = jnp.dot(kt, qit_ref[h * IDX_DIM:(h + 1) * IDX_DIM, :], preferred_element_type=F32)
            acc = acc + wit_ref[h:h + 1, :] * jnp.maximum(s, 0.0)
        return acc

    key_i = lax.broadcasted_iota(I32, (t, t), 0)
    qry_i = lax.broadcasted_iota(I32, (t, t), 1)
    causal = key_i <= qry_i
    diag = jnp.where(causal, score_tile(j), NEG)
    sc_ref[j] = diag
    gmax_ref[...] = diag

    def far_score(c, carry):
        acc = score_tile(c)
        sc_ref[c] = acc
        gmax_ref[...] = jnp.maximum(gmax_ref[...], acc)
        return carry

    lax.fori_loop(0, j, far_score, 0)

    @pl.when((j + 1) % 2 == 1)
    def _():
        sc_ref[j + 1] = jnp.full((t, t), NEG, F32)

    gm = gmax_ref[...]
    lo0 = _f32_to_key(jnp.min(gm, axis=0, keepdims=True))
    hi0 = _f32_to_key(jnp.max(gm, axis=0, keepdims=True)) + 1

    def count_ge(cand):
        def count_pair(pi, cnts):
            cnts = list(cnts)
            for u in range(2):
                for k in range(t // SUBLANES):
                    slab = sc_ref[2 * pi + u, k * SUBLANES:(k + 1) * SUBLANES, :]
                    a = k % len(cnts)
                    cnts[a] = jnp.where(slab >= cand, cnts[a] + 1.0, cnts[a])
            return tuple(cnts)

        cnts = lax.fori_loop(0, (j + 2) // 2, count_pair, tuple(jnp.zeros((SUBLANES, t), F32) for _ in range(4)))
        return jnp.sum((cnts[0] + cnts[1]) + (cnts[2] + cnts[3]), axis=0, keepdims=True)

    def bisect(it, state):
        lo, hi, n_lo = state
        mid = lo + lax.shift_right_logical(hi - lo, 1)
        total = count_ge(_key_to_f32(mid))
        ge = total >= float(topk)
        return jnp.where(ge, mid, lo), jnp.where(ge, hi, mid), jnp.where(ge, total, n_lo)

    steps = jnp.max(32 - lax.clz(hi0 - lo0 - 1))
    unknown = jnp.full((1, t), float(2 ** 30), F32)
    lo, _, n_lo = lax.fori_loop(0, steps, bisect, (lo0, hi0, unknown))
    thr = _key_to_f32(lo)

    @pl.when(jnp.max(n_lo) > float(topk))
    def _():
        def tile_count(pred):
            def body(c, acc):
                hit = pred(sc_ref[c], key_i + c * t)
                return acc + jnp.sum(jnp.where(hit, 1.0, 0.0), axis=0, keepdims=True)
            return lax.fori_loop(0, j + 1, body, jnp.zeros((1, t), F32))

        surplus = tile_count(lambda s, i: s >= thr) > float(topk)
        need = float(topk) - tile_count(lambda s, i: s > thr)

        def cut_step(it, state):
            below, upto = state
            mid = (below + upto) >> 1
            ok = tile_count(lambda s, i: (s == thr) & (i <= mid)) >= need
            return jnp.where(ok, below, mid), jnp.where(ok, mid, upto)

        last = (j + 1) * t - 1
        _, cutoff = lax.fori_loop(0, idx_steps, cut_step,
                                  (jnp.full((1, t), -1, I32), jnp.full((1, t), last, I32)))

        def demote(c, carry):
            s = sc_ref[c]
            drop = surplus & (s == thr) & (key_i + c * t > cutoff)
            sc_ref[c] = jnp.where(drop, NEG, s)
            return carry

        lax.fori_loop(0, j + 1, demote, 0)

    for g in range(N_KV):
        for r in range(GROUP):
            h = g * GROUP + r
            qg_ref[g, r * t:(r + 1) * t, :] = q_ref[:, h * HEAD_DIM:(h + 1) * HEAD_DIM]
    m_ref[...] = jnp.full(m_ref.shape, NEG, F32)
    l_ref[...] = jnp.zeros(l_ref.shape, F32)
    acc_ref[...] = jnp.zeros(acc_ref.shape, F32)
    def bias_rows(h, kind, q0, rc):
        upper, off = divmod(q0, LANES)
        d_blk = tab_ref[h, 0, off:off + rc, :]
        p_blk = tab_ref[h, 1, off:off + rc, :]
        zero_blk = jnp.zeros((rc, LANES), F32)
        if kind == "diag":
            return jnp.concatenate([p_blk, d_blk] if upper else [d_blk, zero_blk], axis=1)
        return None if upper else jnp.concatenate([zero_blk, p_blk], axis=1)

    def attend(c, ntile, kind):
        width = ntile * t
        rc = (LANES * t) // width // 2
        start = pl.multiple_of(c * t, t)
        caps = []
        for u in range(ntile):
            cap_kq = jnp.where(sc_ref[c + u] >= thr, F32_MAX, NEG)
            if kind == "diag":
                cap_kq = jnp.where(causal, cap_kq, NEG)
            caps.append(cap_kq.T)
        cap = jnp.concatenate(caps, axis=1) if ntile > 1 else caps[0]
        def logits(g):
            kt = k_ref[pl.ds(start, width), g * HEAD_DIM:(g + 1) * HEAD_DIM]
            return lax.dot_general(qg_ref[g], kt, (((1,), (1,)), ((), ())), preferred_element_type=F32)

        s_next = logits(0)
        for g in range(N_KV):
            lo = g * HEAD_DIM
            vt = v_ref[pl.ds(start, width), lo:lo + HEAD_DIM]
            s_all = s_next
            if g + 1 < N_KV:
                s_next = logits(g + 1)
            probs, alphas = [], []
            for ch in range(rows // rc):
                rs = slice(ch * rc, (ch + 1) * rc)
                r, q0 = divmod(ch * rc, t)
                s = s_all[rs]
                if kind != "far":
                    b = bias_rows(g * GROUP + r, kind, q0, rc)
                    if b is not None:
                        s = s + b
                s = jnp.minimum(s, cap[q0:q0 + rc])
                m_prev = m_ref[g, rs]
                m_next = jnp.maximum(m_prev, jnp.max(s, axis=1, keepdims=True))
                alpha = jnp.exp2(m_prev - m_next)
                p = jnp.exp2(s - jnp.concatenate([m_next] * (width // LANES), axis=1))
                psum = p[:, :LANES]
                for w in range(1, width // LANES):
                    psum = psum + p[:, w * LANES:(w + 1) * LANES]
                l_ref[g, rs] = alpha * l_ref[g, rs] + psum
                m_ref[g, rs] = m_next
                probs.append(p.astype(BF16))
                alphas.append(alpha)
            pv = jnp.dot(jnp.concatenate(probs, axis=0), vt, preferred_element_type=F32)
            acc_ref[g] = jnp.concatenate(alphas, axis=0) * acc_ref[g] + pv

    n_far = jnp.maximum(j - 1, 0)

    def far_pair(c2, carry):
        attend(2 * c2, 2, "far")
        return carry

    lax.fori_loop(0, n_far // 2, far_pair, 0)

    @pl.when(n_far % 2 == 1)
    def _():
        attend(n_far - 1, 1, "far")

    @pl.when(j >= 1)
    def _():
        attend(j - 1, 1, "prev")

    attend(j, 1, "diag")
    for g in range(N_KV):
        out = acc_ref[g] / jnp.sum(l_ref[g], axis=1, keepdims=True)
        for r in range(GROUP):
            h = g * GROUP + r
            o_ref[:, h * HEAD_DIM:(h + 1) * HEAD_DIM] = out[r * t:(r + 1) * t].astype(o_ref.dtype)


def dsa_attention(proj, qit, wit, tab, batch, topk):
    tt = proj.shape[0]
    s = tt // batch
    t = DSA_T
    nq = s // t
    a_q = N_HEADS * HEAD_DIM
    a_kv = N_KV * HEAD_DIM
    a_qi = IDX_HEADS * IDX_DIM
    k_blk = a_q // a_kv
    ki_blk = (a_q + 2 * a_kv) // LANES
    rows = GROUP * t
    return pl.pallas_call(
        functools.partial(_dsa_kernel, topk=topk, idx_steps=math.ceil(math.log2(s + 1))),
        out_shape=jax.ShapeDtypeStruct((tt, a_q), BF16),
        grid=(batch, nq),
        in_specs=[
            pl.BlockSpec((t, a_q), lambda b, j: (b * nq + j, 0)),
            _resident((s, a_kv), lambda b, j: (b, k_blk)),
            _resident((s, a_kv), lambda b, j: (b, k_blk + 1)),
            pl.BlockSpec((a_qi, t), lambda b, j: (0, b * nq + j)),
            _resident((s, LANES), lambda b, j: (b, ki_blk)),
            pl.BlockSpec((IDX_HEADS, t), lambda b, j: (0, b * nq + j)),
            _resident((N_HEADS, 2, LANES, LANES), lambda b, j: (0, 0, 0, 0)),
        ],
        out_specs=pl.BlockSpec((t, a_q), lambda b, j: (b * nq + j, 0)),
        scratch_shapes=[
            pltpu.VMEM((nq, t, t), F32),
            pltpu.VMEM((t, t), F32),
            pltpu.VMEM((N_KV, rows, HEAD_DIM), BF16),
            pltpu.VMEM((N_KV, rows, LANES), F32),
            pltpu.VMEM((N_KV, rows, LANES), F32),
            pltpu.VMEM((N_KV, rows, HEAD_DIM), F32),
        ],
        compiler_params=_params("parallel", "arbitrary"),
        name="dsa_attention",
    )(proj, proj, proj, qit, proj, wit, tab)


def kernel(x, p, w_in_a, w_out_a, w_q_b, w_out_b, sinks, g_kv, w_kv, rel_bias,
           g_attn, g_ffn, w1, w3, w2, g_pe, w_pe, w_pg, g_final):
    batch, seq, d = x.shape
    depth = g_attn.shape[0]
    n_a = w_in_a.shape[0]
    tt = batch * seq
    a_q = N_HEADS * HEAD_DIM
    a_kv = N_KV * HEAD_DIM
    a_qi = IDX_HEADS * IDX_DIM
    n_main = a_q + 2 * a_kv + a_qi
    topk = min(TOPK_MAX, seq // 4)
    assert DSA_T == 2 * LANES and seq % (2 * DSA_T) == 0 and topk <= DSA_T and tt % TM == 0

    attn_scale = HEAD_DIM ** -0.5
    log2e = math.log2(math.e)
    n_row = a_q + 2 * a_kv + LANES
    scale_row = jnp.concatenate([
        jnp.full((a_q,), attn_scale * log2e, F32), jnp.ones((n_row - a_q,), F32)])
    scale_q = jnp.full((a_q,), attn_scale, F32)

    def bias_table(dist, values):
        onehot = jnp.asarray(_t5_bucket_np(dist))[..., None] == jnp.arange(N_BUCKETS)
        return jnp.einsum("...b,bh->h...", onehot.astype(F32), values, precision=lax.Precision.HIGHEST)

    r = np.arange(LANES)
    dist_diag = r[:, None] - r[None, :]
    dist_prev = dist_diag + LANES
    far_bucket = N_BUCKETS - 1
    assert (_t5_bucket_np(np.arange(LANES, 4 * LANES)) == far_bucket).all()
    rb = rel_bias.astype(F32)
    rb_shift = (rb - rb[far_bucket][None, :]) * log2e
    dsa_tab = bias_table(np.stack([dist_diag, dist_prev]), rb_shift)
    swa_tab = bias_table(r[:, None] + SWA_TQ - np.arange(2 * SWA_TQ)[None, :], rb)

    n_qkv = a_q + 2 * a_kv
    w_in_row = jnp.concatenate([
        w_in_a[:, :, :n_qkv], w_in_a[:, :, n_main:n_main + IDX_DIM],
        jnp.zeros(w_in_a.shape[:2] + (LANES - IDX_DIM,), w_in_a.dtype)], axis=2).astype(BF16)
    w_in_t = jnp.concatenate([
        w_in_a[:, :, n_qkv:n_main], w_in_a[:, :, n_main + IDX_DIM:n_main + IDX_DIM + IDX_HEADS]],
        axis=2).transpose(0, 2, 1).astype(BF16)
    w_out_a_b, w_q_b_b, w_out_b_b, w_kv_b = (a.astype(BF16) for a in (w_out_a, w_q_b, w_out_b, w_kv))
    w1_b, w3_b, w2_b, w_pg_b, w_pe_b = (a.astype(BF16) for a in (w1, w3, w2, w_pg, w_pe))

    x2 = x.reshape(tt, d)
    p2 = p.reshape(depth, tt, p.shape[-1])
    kv = None
    for i in range(depth):
        if i < n_a:
            proj = norm_linear(x2, g_attn[i], w_in_row, i, scale_row, BF16, tn=n_row // 5)
            qit, wit = norm_linear_t(x2, g_attn[i], w_in_t, i, a_qi, IDX_DIM ** -0.5, IDX_HEADS ** -0.5)
            attn = dsa_attention(proj, qit, wit, dsa_tab, batch, topk)
            x2 = linear_residual(attn, w_out_a_b, i, x2)
        else:
            jb = i - n_a
            q = norm_linear(x2, g_attn[i], w_q_b_b, jb, scale_q, BF16, tn=1024)
            attn = swa_attention(q, kv, sinks[jb], swa_tab, batch)
            x2 = linear_residual(attn, w_out_b_b, jb, x2)
        x2 = ffn(x2, g_ffn[i], w1_b, w3_b, w2_b, i)
        x2 = ple_gate(x2, g_pe[i], w_pg_b, p2, w_pe_b, i, g_final if i == depth - 1 else None)
        if i == n_a - 1:
            kv = norm_linear(x2, g_kv, w_kv_b, None, None, BF16, tn=1024)
    return x2.reshape(batch, seq, d)
```

```python
import functools
import math

import numpy as np
import jax
import jax.numpy as jnp
from jax import lax
from jax.experimental import pallas as pl
from jax.experimental.pallas import tpu as pltpu

F32 = jnp.float32
BF16 = jnp.bfloat16
I32 = jnp.int32

N_HEADS = 16
HEAD_DIM = 128
N_KV = 4
GROUP = N_HEADS // N_KV
IDX_HEADS = 16
IDX_DIM = 64
TOPK_MAX = 256
WINDOW = 128
N_BUCKETS = 32
MAX_DIST = 128
EPS = 1e-6
NEG = -1e30
INT_MIN = -(2 ** 31)
F32_MAX = float(np.finfo(np.float32).max)

V7X_VMEM_BYTES = 64 * 1024 * 1024
VMEM_LIMIT = V7X_VMEM_BYTES - 8 * 1024 * 1024
LANES = 128
SUBLANES = 8

TM = 512
TM_FFN = 1024
TF = 512
DSA_T = 256
SWA_TQ = 128


def _params(*sem):
    return pltpu.CompilerParams(dimension_semantics=sem, vmem_limit_bytes=VMEM_LIMIT)


def _resident(shape, index_map):
    return pl.BlockSpec(shape, index_map, pipeline_mode=pl.Buffered(1))


def _rms(x, g):
    ms = jnp.mean(x * x, axis=-1, keepdims=True)
    return x * lax.rsqrt(ms + EPS) * g


def _t5_bucket_np(d):
    d = np.maximum(d, 0)
    max_exact = N_BUCKETS // 2
    nf = np.maximum(d, 1).astype(np.float32)
    large = max_exact + (np.log(nf / np.float32(max_exact)) / np.float32(math.log(MAX_DIST / max_exact))
                         * np.float32(N_BUCKETS - max_exact)).astype(np.int32)
    large = np.minimum(large, N_BUCKETS - 1)
    return np.where(d < max_exact, d, large)


def _norm_linear_kernel(*refs, scaled):
    if scaled:
        x_ref, g_ref, w_ref, s_ref, o_ref = refs
    else:
        x_ref, g_ref, w_ref, o_ref = refs
    h = _rms(x_ref[...], g_ref[...]).astype(BF16)
    y = jnp.dot(h, w_ref[...], preferred_element_type=F32)
    if scaled:
        y = y * s_ref[...]
    o_ref[...] = y.astype(o_ref.dtype)


def _weight_spec(layer, block, index_map, resident=False):
    mode = dict(pipeline_mode=pl.Buffered(1)) if resident else {}
    if layer is None:
        return pl.BlockSpec(block, index_map, **mode)
    return pl.BlockSpec((None,) + tuple(block), lambda *i: (layer,) + tuple(index_map(*i)), **mode)


def norm_linear(x, g, w, layer, col_scale, out_dtype):
    t, d = x.shape
    n = w.shape[-1]
    scaled = col_scale is not None
    in_specs = [
        pl.BlockSpec((TM, d), lambda i: (i, 0)),
        pl.BlockSpec((1, d), lambda i: (0, 0)),
        _weight_spec(layer, (d, n), lambda i: (0, 0), resident=True),
    ]
    args = [x, g.reshape(1, d), w]
    if scaled:
        in_specs.append(pl.BlockSpec((1, n), lambda i: (0, 0)))
        args.append(col_scale.reshape(1, n))
    return pl.pallas_call(
        functools.partial(_norm_linear_kernel, scaled=scaled),
        out_shape=jax.ShapeDtypeStruct((t, n), out_dtype),
        grid=(t // TM,),
        in_specs=in_specs,
        out_specs=pl.BlockSpec((TM, n), lambda i: (i, 0)),
        compiler_params=_params("parallel"),
        name="norm_linear",
    )(*args)


def _dsa_project_kernel(x_ref, g_ref, w_ref, s_ref, wt_ref, o_ref, qit_ref, wit_ref, *, n_q, q_scale, w_scale):
    h = _rms(x_ref[...], g_ref[...]).astype(BF16)
    y = jnp.dot(h, w_ref[...], preferred_element_type=F32) * s_ref[...]
    o_ref[...] = y.astype(o_ref.dtype)
    yt = lax.dot_general(wt_ref[...], h, (((1,), (1,)), ((), ())), preferred_element_type=F32)
    qit_ref[...] = (yt[:n_q] * q_scale).astype(qit_ref.dtype)
    wit_ref[...] = yt[n_q:] * w_scale


def dsa_project(x, g, w_row, col_scale, w_t, layer, n_q, q_scale, w_scale):
    t, d = x.shape
    n = w_row.shape[-1]
    nt = w_t.shape[-2]
    return pl.pallas_call(
        functools.partial(_dsa_project_kernel, n_q=n_q, q_scale=q_scale, w_scale=w_scale),
        out_shape=(jax.ShapeDtypeStruct((t, n), BF16), jax.ShapeDtypeStruct((n_q, t), BF16),
                   jax.ShapeDtypeStruct((nt - n_q, t), F32)),
        grid=(t // TM,),
        in_specs=[
            pl.BlockSpec((TM, d), lambda i: (i, 0)),
            pl.BlockSpec((1, d), lambda i: (0, 0)),
            _weight_spec(layer, (d, n), lambda i: (0, 0), resident=True),
            pl.BlockSpec((1, n), lambda i: (0, 0)),
            _weight_spec(layer, (nt, d), lambda i: (0, 0), resident=True),
        ],
        out_specs=(pl.BlockSpec((TM, n), lambda i: (i, 0)), pl.BlockSpec((n_q, TM), lambda i: (0, i)),
                   pl.BlockSpec((nt - n_q, TM), lambda i: (0, i))),
        compiler_params=_params("parallel"),
        name="dsa_project",
    )(x, g.reshape(1, d), w_row, col_scale.reshape(1, n), w_t)


def _linear_residual_kernel(a_ref, w_ref, r_ref, o_ref):
    o_ref[...] = r_ref[...] + jnp.dot(a_ref[...], w_ref[...], preferred_element_type=F32)


def linear_residual(a, w, layer, res):
    t, k = a.shape
    n = w.shape[-1]
    return pl.pallas_call(
        _linear_residual_kernel,
        out_shape=jax.ShapeDtypeStruct((t, n), F32),
        grid=(t // TM,),
        in_specs=[
            pl.BlockSpec((TM, k), lambda i: (i, 0)),
            _weight_spec(layer, (k, n), lambda i: (0, 0), resident=True),
            pl.BlockSpec((TM, n), lambda i: (i, 0)),
        ],
        out_specs=pl.BlockSpec((TM, n), lambda i: (i, 0)),
        compiler_params=_params("parallel"),
        name="linear_residual",
    )(a, w, res)


def _ffn_kernel(x_ref, g_ref, w1_ref, w3_ref, w2_ref, o_ref, h_ref):
    @pl.when(pl.program_id(1) == 0)
    def _():
        x = x_ref[...]
        h_ref[...] = _rms(x, g_ref[...]).astype(BF16)
        o_ref[...] = x

    h = h_ref[...]
    a = jnp.dot(h, w1_ref[...], preferred_element_type=F32)
    b = jnp.dot(h, w3_ref[...], preferred_element_type=F32)
    u = (a * jax.nn.sigmoid(a) * b).astype(BF16)
    o_ref[...] += jnp.dot(u, w2_ref[...], preferred_element_type=F32)


def ffn(x, g, w1, w3, w2, layer):
    t, d = x.shape
    f = w1.shape[-1]
    return pl.pallas_call(
        _ffn_kernel,
        out_shape=jax.ShapeDtypeStruct((t, d), F32),
        grid=(t // TM_FFN, f // TF),
        in_specs=[
            pl.BlockSpec((TM_FFN, d), lambda i, j: (i, 0)),
            pl.BlockSpec((1, d), lambda i, j: (0, 0)),
            _weight_spec(layer, (d, TF), lambda i, j: (0, j)),
            _weight_spec(layer, (d, TF), lambda i, j: (0, j)),
            _weight_spec(layer, (TF, d), lambda i, j: (j, 0)),
        ],
        out_specs=pl.BlockSpec((TM_FFN, d), lambda i, j: (i, 0)),
        scratch_shapes=[pltpu.VMEM((TM_FFN, d), BF16)],
        compiler_params=_params("parallel", "arbitrary"),
        name="ffn",
    )(x, g.reshape(1, d), w1, w3, w2)


def _ple_kernel(*refs, final):
    if final:
        x_ref, g_ref, wpg_ref, p_ref, wpe_ref, gf_ref, o_ref = refs
    else:
        x_ref, g_ref, wpg_ref, p_ref, wpe_ref, o_ref = refs
    x = x_ref[...]
    h = _rms(x, g_ref[...]).astype(BF16)
    gate = jax.nn.sigmoid(jnp.dot(h, wpg_ref[...], preferred_element_type=F32))
    e = jnp.dot(p_ref[...].astype(BF16), wpe_ref[...], preferred_element_type=F32)
    y = x + e * gate
    if final:
        y = _rms(y, gf_ref[...])
    o_ref[...] = y


def ple_gate(x, g, w_pg, p, w_pe, layer, g_final):
    t, d = x.shape
    pd = p.shape[-1]
    final = g_final is not None
    in_specs = [
        pl.BlockSpec((TM, d), lambda i: (i, 0)),
        pl.BlockSpec((1, d), lambda i: (0, 0)),
        _weight_spec(layer, (d, d), lambda i: (0, 0), resident=True),
        _weight_spec(layer, (TM, pd), lambda i: (i, 0)),
        _weight_spec(layer, (pd, d), lambda i: (0, 0), resident=True),
    ]
    args = [x, g.reshape(1, d), w_pg, p, w_pe]
    if final:
        in_specs.append(pl.BlockSpec((1, d), lambda i: (0, 0)))
        args.append(g_final.reshape(1, d))
    return pl.pallas_call(
        functools.partial(_ple_kernel, final=final),
        out_shape=jax.ShapeDtypeStruct((t, d), F32),
        grid=(t // TM,),
        in_specs=in_specs,
        out_specs=pl.BlockSpec((TM, d), lambda i: (i, 0)),
        compiler_params=_params("parallel"),
        name="ple_gate",
    )(*args)


def _swa_kernel(sink_ref, q_ref, kp_ref, kc_ref, vp_ref, vc_ref, tab_ref, o_ref):
    n = pl.program_id(1)
    tq = SWA_TQ
    row = lax.broadcasted_iota(I32, (tq, 2 * tq), 0)
    col = lax.broadcasted_iota(I32, (tq, 2 * tq), 1)
    dist = row + tq - col
    valid = (dist >= 0) & (dist < WINDOW) & ((col >= tq) | (n > 0))
    for g in range(N_KV):
        lo = g * HEAD_DIM
        kg = jnp.concatenate([kp_ref[:, lo:lo + HEAD_DIM], kc_ref[:, lo:lo + HEAD_DIM]], axis=0)
        vg = jnp.concatenate([vp_ref[:, lo:lo + HEAD_DIM], vc_ref[:, lo:lo + HEAD_DIM]], axis=0)
        qg = jnp.concatenate(
            [q_ref[:, (g * GROUP + r) * HEAD_DIM:(g * GROUP + r + 1) * HEAD_DIM] for r in range(GROUP)], axis=0)
        s_all = lax.dot_general(qg, kg, (((1,), (1,)), ((), ())), preferred_element_type=F32)
        probs = []
        for r in range(GROUP):
            h = g * GROUP + r
            s = s_all[r * tq:(r + 1) * tq] + tab_ref[h]
            s = jnp.where(valid, s, NEG)
            sink = sink_ref[h]
            m = jnp.maximum(jnp.max(s, axis=1, keepdims=True), sink)
            e = jnp.exp(s - m)
            denom = jnp.sum(e, axis=1, keepdims=True) + jnp.exp(sink - m)
            probs.append((e / denom).astype(BF16))
        pg = jnp.concatenate(probs, axis=0)
        og = jnp.dot(pg, vg, preferred_element_type=F32)
        for r in range(GROUP):
            h = g * GROUP + r
            o_ref[:, h * HEAD_DIM:(h + 1) * HEAD_DIM] = og[r * tq:(r + 1) * tq].astype(o_ref.dtype)


def swa_attention(q, kv, sinks, tab, batch):
    t, a_q = q.shape
    a_kv = N_KV * HEAD_DIM
    tq = SWA_TQ
    nb = t // batch // tq

    def prev_k(b, n):
        return (jnp.maximum(b * nb + n - 1, 0), 0)

    def prev_v(b, n):
        return (jnp.maximum(b * nb + n - 1, 0), 1)

    return pl.pallas_call(
        _swa_kernel,
        out_shape=jax.ShapeDtypeStruct((t, a_q), BF16),
        grid=(batch, nb),
        in_specs=[
            pl.BlockSpec(memory_space=pltpu.SMEM),
            pl.BlockSpec((tq, a_q), lambda b, n: (b * nb + n, 0)),
            pl.BlockSpec((tq, a_kv), prev_k),
            pl.BlockSpec((tq, a_kv), lambda b, n: (b * nb + n, 0)),
            pl.BlockSpec((tq, a_kv), prev_v),
            pl.BlockSpec((tq, a_kv), lambda b, n: (b * nb + n, 1)),
            _resident((N_HEADS, tq, 2 * tq), lambda b, n: (0, 0, 0)),
        ],
        out_specs=pl.BlockSpec((tq, a_q), lambda b, n: (b * nb + n, 0)),
        compiler_params=_params("parallel", "parallel"),
        name="swa_attention",
    )(sinks, q, kv, kv, kv, kv, tab)


def _key_to_f32(key):
    bits = jnp.where(key < 0, key ^ jnp.int32(0x7FFFFFFF), key)
    return pltpu.bitcast(bits, F32)


def _f32_to_key(x):
    bits = pltpu.bitcast(x, I32)
    return jnp.where(bits < 0, bits ^ jnp.int32(0x7FFFFFFF), bits)


def _dsa_kernel(q_ref, k_ref, v_ref, qit_ref, ki_ref, wit_ref, tab_ref, o_ref,
                sc_ref, gmax_ref, qg_ref, m_ref, l_ref, acc_ref, *, topk, idx_steps):
    t = DSA_T
    j = pl.program_id(1)
    rows = GROUP * t

    def score_tile(c):
        kt = ki_ref[pl.ds(pl.multiple_of(c * t, t), t), :IDX_DIM]
        acc = jnp.zeros((t, t), F32)
        for h in range(IDX_HEADS):
            s = jnp.dot(kt, qit_ref[h * IDX_DIM:(h + 1) * IDX_DIM, :], preferred_element_type=F32)
            acc = acc + wit_ref[h:h + 1, :] * jnp.maximum(s, 0.0)
        return acc

    key_i = lax.broadcasted_iota(I32, (t, t), 0)
    qry_i = lax.broadcasted_iota(I32, (t, t), 1)
    causal = key_i <= qry_i
    diag = jnp.where(causal, score_tile(j), NEG)
    sc_ref[j] = diag
    gmax_ref[...] = diag

    def far_score(c, carry):
        acc = score_tile(c)
        sc_ref[c] = acc
        gmax_ref[...] = jnp.maximum(gmax_ref[...], acc)
        return carry

    lax.fori_loop(0, j, far_score, 0)

    @pl.when((j + 1) % 2 == 1)
    def _():
        sc_ref[j + 1] = jnp.full((t, t), NEG, F32)

    gm = gmax_ref[...]
    lo0 = _f32_to_key(jnp.min(gm, axis=0, keepdims=True))
    hi0 = _f32_to_key(jnp.max(gm, axis=0, keepdims=True)) + 1

    def count_ge(cand):
        def count_pair(pi, cnts):
            cnts = list(cnts)
            for u in range(2):
                for k in range(t // SUBLANES):
                    slab = sc_ref[2 * pi + u, k * SUBLANES:(k + 1) * SUBLANES, :]
                    a = k % len(cnts)
                    cnts[a] = jnp.where(slab >= cand, cnts[a] + 1.0, cnts[a])
            return tuple(cnts)

        cnts = lax.fori_loop(0, (j + 2) // 2, count_pair, tuple(jnp.zeros((SUBLANES, t), F32) for _ in range(4)))
        return jnp.sum((cnts[0] + cnts[1]) + (cnts[2] + cnts[3]), axis=0, keepdims=True)

    def bisect(it, state):
        lo, hi, n_lo = state
        mid = lo + lax.shift_right_logical(hi - lo, 1)
        total = count_ge(_key_to_f32(mid))
        ge = total >= float(topk)
        return jnp.where(ge, mid, lo), jnp.where(ge, hi, mid), jnp.where(ge, total, n_lo)

    steps = jnp.max(32 - lax.clz(hi0 - lo0 - 1))
    unknown = jnp.full((1, t), float(2 ** 30), F32)
    lo, _, n_lo = lax.fori_loop(0, steps, bisect, (lo0, hi0, unknown))
    thr = _key_to_f32(lo)

    @pl.when(jnp.max(n_lo) > float(topk))
    def _():
        def tile_count(pred):
            def body(c, acc):
                hit = pred(sc_ref[c], key_i + c * t)
                return acc + jnp.sum(jnp.where(hit, 1.0, 0.0), axis=0, keepdims=True)
            return lax.fori_loop(0, j + 1, body, jnp.zeros((1, t), F32))

        surplus = tile_count(lambda s, i: s >= thr) > float(topk)
        need = float(topk) - tile_count(lambda s, i: s > thr)

        def cut_step(it, state):
            below, upto = state
            mid = (below + upto) >> 1
            ok = tile_count(lambda s, i: (s == thr) & (i <= mid)) >= need
            return jnp.where(ok, below, mid), jnp.where(ok, mid, upto)

        last = (j + 1) * t - 1
        _, cutoff = lax.fori_loop(0, idx_steps, cut_step,
                                  (jnp.full((1, t), -1, I32), jnp.full((1, t), last, I32)))

        def demote(c, carry):
            s = sc_ref[c]
            drop = surplus & (s == thr) & (key_i + c * t > cutoff)
            sc_ref[c] = jnp.where(drop, NEG, s)
            return carry

        lax.fori_loop(0, j + 1, demote, 0)

    for g in range(N_KV):
        for r in range(GROUP):
            h = g * GROUP + r
            qg_ref[g, r * t:(r + 1) * t, :] = q_ref[:, h * HEAD_DIM:(h + 1) * HEAD_DIM]
    m_ref[...] = jnp.full(m_ref.shape, NEG, F32)
    l_ref[...] = jnp.zeros(l_ref.shape, F32)
    acc_ref[...] = jnp.zeros(acc_ref.shape, F32)
    def bias_rows(h, kind, q0, rc):
        upper, off = divmod(q0, LANES)
        d_blk = tab_ref[h, 0, off:off + rc, :]
        p_blk = tab_ref[h, 1, off:off + rc, :]
        zero_blk = jnp.zeros((rc, LANES), F32)
        own = [p_blk, d_blk] if upper else [d_blk, zero_blk]
        if kind == "diag":
            return jnp.concatenate(own, axis=1)
        return jnp.concatenate([zero_blk, zero_blk if upper else p_blk] + own, axis=1)

    def attend(c, ntile, kind):
        width = ntile * t
        rc = (LANES * t) // width // 2
        start = pl.multiple_of(c * t, t)
        caps = []
        for u in range(ntile):
            cap_kq = jnp.where(sc_ref[c + u] >= thr, F32_MAX, NEG)
            if kind != "far" and u == ntile - 1:
                cap_kq = jnp.where(causal, cap_kq, NEG)
            caps.append(cap_kq.T)
        cap = jnp.concatenate(caps, axis=1) if ntile > 1 else caps[0]
        def logits(g):
            kt = k_ref[pl.ds(start, width), g * HEAD_DIM:(g + 1) * HEAD_DIM]
            return lax.dot_general(qg_ref[g], kt, (((1,), (1,)), ((), ())), preferred_element_type=F32)

        s_next = logits(0)
        for g in range(N_KV):
            lo = g * HEAD_DIM
            vt = v_ref[pl.ds(start, width), lo:lo + HEAD_DIM]
            s_all = s_next
            if g + 1 < N_KV:
                s_next = logits(g + 1)
            probs, alphas = [], []
            for ch in range(rows // rc):
                rs = slice(ch * rc, (ch + 1) * rc)
                r, q0 = divmod(ch * rc, t)
                s = s_all[rs]
                if kind != "far":
                    s = s + bias_rows(g * GROUP + r, kind, q0, rc)
                s = jnp.minimum(s, cap[q0:q0 + rc])
                m_prev = m_ref[g, rs]
                m_next = jnp.maximum(m_prev, jnp.max(s, axis=1, keepdims=True))
                alpha = jnp.exp2(m_prev - m_next)
                p = jnp.exp2(s - jnp.concatenate([m_next] * (width // LANES), axis=1))
                psum = p[:, :LANES]
                for w in range(1, width // LANES):
                    psum = psum + p[:, w * LANES:(w + 1) * LANES]
                l_ref[g, rs] = alpha * l_ref[g, rs] + psum
                m_ref[g, rs] = m_next
                probs.append(p.astype(BF16))
                alphas.append(alpha)
            pv = jnp.dot(jnp.concatenate(probs, axis=0), vt, preferred_element_type=F32)
            acc_ref[g] = jnp.concatenate(alphas, axis=0) * acc_ref[g] + pv

    n_far = jnp.maximum(j - 1, 0)

    def far_pair(c2, carry):
        attend(2 * c2, 2, "far")
        return carry

    lax.fori_loop(0, n_far // 2, far_pair, 0)

    @pl.when(n_far % 2 == 1)
    def _():
        attend(n_far - 1, 1, "far")

    @pl.when(j >= 1)
    def _():
        attend(j - 1, 2, "near")

    @pl.when(j == 0)
    def _():
        attend(0, 1, "diag")
    for g in range(N_KV):
        out = acc_ref[g] / jnp.sum(l_ref[g], axis=1, keepdims=True)
        for r in range(GROUP):
            h = g * GROUP + r
            o_ref[:, h * HEAD_DIM:(h + 1) * HEAD_DIM] = out[r * t:(r + 1) * t].astype(o_ref.dtype)


def dsa_attention(proj, qit, wit, tab, batch, topk):
    tt = proj.shape[0]
    s = tt // batch
    t = DSA_T
    nq = s // t
    a_q = N_HEADS * HEAD_DIM
    a_kv = N_KV * HEAD_DIM
    a_qi = IDX_HEADS * IDX_DIM
    k_blk = a_q // a_kv
    ki_blk = (a_q + 2 * a_kv) // LANES
    rows = GROUP * t
    return pl.pallas_call(
        functools.partial(_dsa_kernel, topk=topk, idx_steps=math.ceil(math.log2(s + 1))),
        out_shape=jax.ShapeDtypeStruct((tt, a_q), BF16),
        grid=(batch, nq),
        in_specs=[
            pl.BlockSpec((t, a_q), lambda b, j: (b * nq + j, 0)),
            _resident((s, a_kv), lambda b, j: (b, k_blk)),
            _resident((s, a_kv), lambda b, j: (b, k_blk + 1)),
            pl.BlockSpec((a_qi, t), lambda b, j: (0, b * nq + j)),
            _resident((s, LANES), lambda b, j: (b, ki_blk)),
            pl.BlockSpec((IDX_HEADS, t), lambda b, j: (0, b * nq + j)),
            _resident((N_HEADS, 2, LANES, LANES), lambda b, j: (0, 0, 0, 0)),
        ],
        out_specs=pl.BlockSpec((t, a_q), lambda b, j: (b * nq + j, 0)),
        scratch_shapes=[
            pltpu.VMEM((nq, t, t), F32),
            pltpu.VMEM((t, t), F32),
            pltpu.VMEM((N_KV, rows, HEAD_DIM), BF16),
            pltpu.VMEM((N_KV, rows, LANES), F32),
            pltpu.VMEM((N_KV, rows, LANES), F32),
            pltpu.VMEM((N_KV, rows, HEAD_DIM), F32),
        ],
        compiler_params=_params("parallel", "arbitrary"),
        name="dsa_attention",
    )(proj, proj, proj, qit, proj, wit, tab)


def kernel(x, p, w_in_a, w_out_a, w_q_b, w_out_b, sinks, g_kv, w_kv, rel_bias,
           g_attn, g_ffn, w1, w3, w2, g_pe, w_pe, w_pg, g_final):
    batch, seq, d = x.shape
    depth = g_attn.shape[0]
    n_a = w_in_a.shape[0]
    tt = batch * seq
    a_q = N_HEADS * HEAD_DIM
    a_kv = N_KV * HEAD_DIM
    a_qi = IDX_HEADS * IDX_DIM
    n_main = a_q + 2 * a_kv + a_qi
    topk = min(TOPK_MAX, seq // 4)
    assert DSA_T == 2 * LANES and seq % (2 * DSA_T) == 0 and topk <= DSA_T and tt % TM == 0

    attn_scale = HEAD_DIM ** -0.5
    log2e = math.log2(math.e)
    n_row = a_q + 2 * a_kv + LANES
    scale_row = jnp.concatenate([
        jnp.full((a_q,), attn_scale * log2e, F32), jnp.ones((n_row - a_q,), F32)])
    scale_q = jnp.full((a_q,), attn_scale, F32)

    def bias_table(dist, values):
        onehot = jnp.asarray(_t5_bucket_np(dist))[..., None] == jnp.arange(N_BUCKETS)
        return jnp.einsum("...b,bh->h...", onehot.astype(F32), values, precision=lax.Precision.HIGHEST)

    r = np.arange(LANES)
    dist_diag = r[:, None] - r[None, :]
    dist_prev = dist_diag + LANES
    far_bucket = N_BUCKETS - 1
    assert (_t5_bucket_np(np.arange(LANES, 4 * LANES)) == far_bucket).all()
    rb = rel_bias.astype(F32)
    rb_shift = (rb - rb[far_bucket][None, :]) * log2e
    dsa_tab = bias_table(np.stack([dist_diag, dist_prev]), rb_shift)
    swa_tab = bias_table(r[:, None] + SWA_TQ - np.arange(2 * SWA_TQ)[None, :], rb)

    n_qkv = a_q + 2 * a_kv
    w_in_row = jnp.concatenate([
        w_in_a[:, :, :n_qkv], w_in_a[:, :, n_main:n_main + IDX_DIM],
        jnp.zeros(w_in_a.shape[:2] + (LANES - IDX_DIM,), w_in_a.dtype)], axis=2).astype(BF16)
    w_in_t = jnp.concatenate([
        w_in_a[:, :, n_qkv:n_main], w_in_a[:, :, n_main + IDX_DIM:n_main + IDX_DIM + IDX_HEADS]],
        axis=2).transpose(0, 2, 1).astype(BF16)
    w_out_a_b, w_q_b_b, w_out_b_b, w_kv_b = (a.astype(BF16) for a in (w_out_a, w_q_b, w_out_b, w_kv))
    w1_b, w3_b, w2_b, w_pg_b, w_pe_b = (a.astype(BF16) for a in (w1, w3, w2, w_pg, w_pe))

    x2 = x.reshape(tt, d)
    p2 = p.reshape(depth, tt, p.shape[-1])
    kv = None
    for i in range(depth):
        if i < n_a:
            proj, qit, wit = dsa_project(x2, g_attn[i], w_in_row, scale_row, w_in_t, i, a_qi,
                                         IDX_DIM ** -0.5, IDX_HEADS ** -0.5)
            attn = dsa_attention(proj, qit, wit, dsa_tab, batch, topk)
            x2 = linear_residual(attn, w_out_a_b, i, x2)
        else:
            jb = i - n_a
            q = norm_linear(x2, g_attn[i], w_q_b_b, jb, scale_q, BF16)
            attn = swa_attention(q, kv, sinks[jb], swa_tab, batch)
            x2 = linear_residual(attn, w_out_b_b, jb, x2)
        x2 = ffn(x2, g_ffn[i], w1_b, w3_b, w2_b, i)
        x2 = ple_gate(x2, g_pe[i], w_pg_b, p2, w_pe_b, i, g_final if i == depth - 1 else None)
        if i == n_a - 1:
            kv = norm_linear(x2, g_kv, w_kv_b, None, None, BF16)
    return x2.reshape(batch, seq, d)
```

```python
import functools
import math

import numpy as np
import jax
import jax.numpy as jnp
from jax import lax
from jax.experimental import pallas as pl
from jax.experimental.pallas import tpu as pltpu

F32 = jnp.float32
BF16 = jnp.bfloat16
I32 = jnp.int32
I16 = jnp.int16

N_HEADS = 16
HEAD_DIM = 128
N_KV = 4
GROUP = N_HEADS // N_KV
IDX_HEADS = 16
IDX_DIM = 64
TOPK_MAX = 256
WINDOW = 128
N_BUCKETS = 32
MAX_DIST = 128
EPS = 1e-6
NEG = -1e30
F32_MAX = float(np.finfo(np.float32).max)
HALF_BITS = 16
LOW_MASK = 2 ** HALF_BITS - 1
LOW_BIAS = 2 ** (HALF_BITS - 1)
I16_MIN = -(2 ** (HALF_BITS - 1))

V7X_VMEM_BYTES = 64 * 1024 * 1024
VMEM_LIMIT = V7X_VMEM_BYTES - 8 * 1024 * 1024
LANES = 128
SUBLANES = 8

TM = 512
TM_FFN = 1024
TF = 512
DSA_T = 256
SWA_TQ = 128


def _params(*sem):
    return pltpu.CompilerParams(dimension_semantics=sem, vmem_limit_bytes=VMEM_LIMIT)


def _resident(shape, index_map):
    return pl.BlockSpec(shape, index_map, pipeline_mode=pl.Buffered(1))


def _rms(x, g):
    ms = jnp.mean(x * x, axis=-1, keepdims=True)
    return x * lax.rsqrt(ms + EPS) * g


def _t5_bucket_np(d):
    d = np.maximum(d, 0)
    max_exact = N_BUCKETS // 2
    nf = np.maximum(d, 1).astype(np.float32)
    large = max_exact + (np.log(nf / np.float32(max_exact)) / np.float32(math.log(MAX_DIST / max_exact))
                         * np.float32(N_BUCKETS - max_exact)).astype(np.int32)
    large = np.minimum(large, N_BUCKETS - 1)
    return np.where(d < max_exact, d, large)


def _norm_linear_kernel(*refs, scaled):
    if scaled:
        x_ref, g_ref, w_ref, s_ref, o_ref = refs
    else:
        x_ref, g_ref, w_ref, o_ref = refs
    h = _rms(x_ref[...], g_ref[...]).astype(BF16)
    y = jnp.dot(h, w_ref[...], preferred_element_type=F32)
    if scaled:
        y = y * s_ref[...]
    o_ref[...] = y.astype(o_ref.dtype)


def _weight_spec(layer, block, index_map, resident=False):
    mode = dict(pipeline_mode=pl.Buffered(1)) if resident else {}
    if layer is None:
        return pl.BlockSpec(block, index_map, **mode)
    return pl.BlockSpec((None,) + tuple(block), lambda *i: (layer,) + tuple(index_map(*i)), **mode)


def norm_linear(x, g, w, layer, col_scale, out_dtype):
    t, d = x.shape
    n = w.shape[-1]
    scaled = col_scale is not None
    in_specs = [
        pl.BlockSpec((TM, d), lambda i: (i, 0)),
        pl.BlockSpec((1, d), lambda i: (0, 0)),
        _weight_spec(layer, (d, n), lambda i: (0, 0), resident=True),
    ]
    args = [x, g.reshape(1, d), w]
    if scaled:
        in_specs.append(pl.BlockSpec((1, n), lambda i: (0, 0)))
        args.append(col_scale.reshape(1, n))
    return pl.pallas_call(
        functools.partial(_norm_linear_kernel, scaled=scaled),
        out_shape=jax.ShapeDtypeStruct((t, n), out_dtype),
        grid=(t // TM,),
        in_specs=in_specs,
        out_specs=pl.BlockSpec((TM, n), lambda i: (i, 0)),
        compiler_params=_params("parallel"),
        name="norm_linear",
    )(*args)


def _dsa_project_kernel(x_ref, g_ref, w_ref, s_ref, wt_ref, o_ref, qit_ref, wit_ref, *, n_q, q_scale, w_scale):
    h = _rms(x_ref[...], g_ref[...]).astype(BF16)
    y = jnp.dot(h, w_ref[...], preferred_element_type=F32) * s_ref[...]
    o_ref[...] = y.astype(o_ref.dtype)
    yt = lax.dot_general(wt_ref[...], h, (((1,), (1,)), ((), ())), preferred_element_type=F32)
    qit_ref[...] = (yt[:n_q] * q_scale).astype(qit_ref.dtype)
    wit_ref[...] = yt[n_q:] * w_scale


def dsa_project(x, g, w_row, col_scale, w_t, layer, n_q, q_scale, w_scale):
    t, d = x.shape
    n = w_row.shape[-1]
    nt = w_t.shape[-2]
    return pl.pallas_call(
        functools.partial(_dsa_project_kernel, n_q=n_q, q_scale=q_scale, w_scale=w_scale),
        out_shape=(jax.ShapeDtypeStruct((t, n), BF16), jax.ShapeDtypeStruct((n_q, t), BF16),
                   jax.ShapeDtypeStruct((nt - n_q, t), F32)),
        grid=(t // TM,),
        in_specs=[
            pl.BlockSpec((TM, d), lambda i: (i, 0)),
            pl.BlockSpec((1, d), lambda i: (0, 0)),
            _weight_spec(layer, (d, n), lambda i: (0, 0), resident=True),
            pl.BlockSpec((1, n), lambda i: (0, 0)),
            _weight_spec(layer, (nt, d), lambda i: (0, 0), resident=True),
        ],
        out_specs=(pl.BlockSpec((TM, n), lambda i: (i, 0)), pl.BlockSpec((n_q, TM), lambda i: (0, i)),
                   pl.BlockSpec((nt - n_q, TM), lambda i: (0, i))),
        compiler_params=_params("parallel"),
        name="dsa_project",
    )(x, g.reshape(1, d), w_row, col_scale.reshape(1, n), w_t)


def _linear_residual_kernel(a_ref, w_ref, r_ref, o_ref):
    o_ref[...] = r_ref[...] + jnp.dot(a_ref[...], w_ref[...], preferred_element_type=F32)


def linear_residual(a, w, layer, res):
    t, k = a.shape
    n = w.shape[-1]
    return pl.pallas_call(
        _linear_residual_kernel,
        out_shape=jax.ShapeDtypeStruct((t, n), F32),
        grid=(t // TM,),
        in_specs=[
            pl.BlockSpec((TM, k), lambda i: (i, 0)),
            _weight_spec(layer, (k, n), lambda i: (0, 0), resident=True),
            pl.BlockSpec((TM, n), lambda i: (i, 0)),
        ],
        out_specs=pl.BlockSpec((TM, n), lambda i: (i, 0)),
        compiler_params=_params("parallel"),
        name="linear_residual",
    )(a, w, res)


def _ffn_kernel(x_ref, g_ref, w1_ref, w3_ref, w2_ref, o_ref, h_ref):
    @pl.when(pl.program_id(1) == 0)
    def _():
        x = x_ref[...]
        h_ref[...] = _rms(x, g_ref[...]).astype(BF16)
        o_ref[...] = x

    h = h_ref[...]
    a = jnp.dot(h, w1_ref[...], preferred_element_type=F32)
    b = jnp.dot(h, w3_ref[...], preferred_element_type=F32)
    u = (a * jax.nn.sigmoid(a) * b).astype(BF16)
    o_ref[...] += jnp.dot(u, w2_ref[...], preferred_element_type=F32)


def ffn(x, g, w1, w3, w2, layer):
    t, d = x.shape
    f = w1.shape[-1]
    return pl.pallas_call(
        _ffn_kernel,
        out_shape=jax.ShapeDtypeStruct((t, d), F32),
        grid=(t // TM_FFN, f // TF),
        in_specs=[
            pl.BlockSpec((TM_FFN, d), lambda i, j: (i, 0)),
            pl.BlockSpec((1, d), lambda i, j: (0, 0)),
            _weight_spec(layer, (d, TF), lambda i, j: (0, j)),
            _weight_spec(layer, (d, TF), lambda i, j: (0, j)),
            _weight_spec(layer, (TF, d), lambda i, j: (j, 0)),
        ],
        out_specs=pl.BlockSpec((TM_FFN, d), lambda i, j: (i, 0)),
        scratch_shapes=[pltpu.VMEM((TM_FFN, d), BF16)],
        compiler_params=_params("parallel", "arbitrary"),
        name="ffn",
    )(x, g.reshape(1, d), w1, w3, w2)


def _ple_kernel(*refs, final):
    if final:
        x_ref, g_ref, wpg_ref, p_ref, wpe_ref, gf_ref, o_ref = refs
    else:
        x_ref, g_ref, wpg_ref, p_ref, wpe_ref, o_ref = refs
    x = x_ref[...]
    h = _rms(x, g_ref[...]).astype(BF16)
    gate = jax.nn.sigmoid(jnp.dot(h, wpg_ref[...], preferred_element_type=F32))
    e = jnp.dot(p_ref[...].astype(BF16), wpe_ref[...], preferred_element_type=F32)
    y = x + e * gate
    if final:
        y = _rms(y, gf_ref[...])
    o_ref[...] = y


def ple_gate(x, g, w_pg, p, w_pe, layer, g_final):
    t, d = x.shape
    pd = p.shape[-1]
    final = g_final is not None
    in_specs = [
        pl.BlockSpec((TM, d), lambda i: (i, 0)),
        pl.BlockSpec((1, d), lambda i: (0, 0)),
        _weight_spec(layer, (d, d), lambda i: (0, 0), resident=True),
        _weight_spec(layer, (TM, pd), lambda i: (i, 0)),
        _weight_spec(layer, (pd, d), lambda i: (0, 0), resident=True),
    ]
    args = [x, g.reshape(1, d), w_pg, p, w_pe]
    if final:
        in_specs.append(pl.BlockSpec((1, d), lambda i: (0, 0)))
        args.append(g_final.reshape(1, d))
    return pl.pallas_call(
        functools.partial(_ple_kernel, final=final),
        out_shape=jax.ShapeDtypeStruct((t, d), F32),
        grid=(t // TM,),
        in_specs=in_specs,
        out_specs=pl.BlockSpec((TM, d), lambda i: (i, 0)),
        compiler_params=_params("parallel"),
        name="ple_gate",
    )(*args)


def _swa_kernel(sink_ref, q_ref, kp_ref, kc_ref, vp_ref, vc_ref, tab_ref, o_ref):
    n = pl.program_id(1)
    tq = SWA_TQ
    row = lax.broadcasted_iota(I32, (tq, 2 * tq), 0)
    col = lax.broadcasted_iota(I32, (tq, 2 * tq), 1)
    dist = row + tq - col
    valid = (dist >= 0) & (dist < WINDOW) & ((col >= tq) | (n > 0))
    for g in range(N_KV):
        lo = g * HEAD_DIM
        kg = jnp.concatenate([kp_ref[:, lo:lo + HEAD_DIM], kc_ref[:, lo:lo + HEAD_DIM]], axis=0)
        vg = jnp.concatenate([vp_ref[:, lo:lo + HEAD_DIM], vc_ref[:, lo:lo + HEAD_DIM]], axis=0)
        qg = jnp.concatenate(
            [q_ref[:, (g * GROUP + r) * HEAD_DIM:(g * GROUP + r + 1) * HEAD_DIM] for r in range(GROUP)], axis=0)
        s_all = lax.dot_general(qg, kg, (((1,), (1,)), ((), ())), preferred_element_type=F32)
        probs = []
        for r in range(GROUP):
            h = g * GROUP + r
            s = s_all[r * tq:(r + 1) * tq] + tab_ref[h]
            s = jnp.where(valid, s, NEG)
            sink = sink_ref[h]
            m = jnp.maximum(jnp.max(s, axis=1, keepdims=True), sink)
            e = jnp.exp(s - m)
            denom = jnp.sum(e, axis=1, keepdims=True) + jnp.exp(sink - m)
            probs.append((e / denom).astype(BF16))
        pg = jnp.concatenate(probs, axis=0)
        og = jnp.dot(pg, vg, preferred_element_type=F32)
        for r in range(GROUP):
            h = g * GROUP + r
            o_ref[:, h * HEAD_DIM:(h + 1) * HEAD_DIM] = og[r * tq:(r + 1) * tq].astype(o_ref.dtype)


def swa_attention(q, kv, sinks, tab, batch):
    t, a_q = q.shape
    a_kv = N_KV * HEAD_DIM
    tq = SWA_TQ
    nb = t // batch // tq

    def prev_k(b, n):
        return (jnp.maximum(b * nb + n - 1, 0), 0)

    def prev_v(b, n):
        return (jnp.maximum(b * nb + n - 1, 0), 1)

    return pl.pallas_call(
        _swa_kernel,
        out_shape=jax.ShapeDtypeStruct((t, a_q), BF16),
        grid=(batch, nb),
        in_specs=[
            pl.BlockSpec(memory_space=pltpu.SMEM),
            pl.BlockSpec((tq, a_q), lambda b, n: (b * nb + n, 0)),
            pl.BlockSpec((tq, a_kv), prev_k),
            pl.BlockSpec((tq, a_kv), lambda b, n: (b * nb + n, 0)),
            pl.BlockSpec((tq, a_kv), prev_v),
            pl.BlockSpec((tq, a_kv), lambda b, n: (b * nb + n, 1)),
            _resident((N_HEADS, tq, 2 * tq), lambda b, n: (0, 0, 0)),
        ],
        out_specs=pl.BlockSpec((tq, a_q), lambda b, n: (b * nb + n, 0)),
        compiler_params=_params("parallel", "parallel"),
        name="swa_attention",
    )(sinks, q, kv, kv, kv, kv, tab)


def _key_to_f32(key):
    bits = jnp.where(key < 0, key ^ jnp.int32(0x7FFFFFFF), key)
    return pltpu.bitcast(bits, F32)


def _f32_to_key(x):
    bits = pltpu.bitcast(x, I32)
    return jnp.where(bits < 0, bits ^ jnp.int32(0x7FFFFFFF), bits)


def _dsa_kernel(q_ref, k_ref, v_ref, qit_ref, ki_ref, wit_ref, tab_ref, o_ref,
                sc_ref, khi_ref, klo_ref, gmax_ref, qg_ref, m_ref, l_ref, acc_ref, *, topk, idx_steps):
    t = DSA_T
    j = pl.program_id(1)
    rows = GROUP * t

    def score_tile(c):
        kt = ki_ref[pl.ds(pl.multiple_of(c * t, t), t), :IDX_DIM]
        acc = jnp.zeros((t, t), F32)
        for h in range(IDX_HEADS):
            s = jnp.dot(kt, qit_ref[h * IDX_DIM:(h + 1) * IDX_DIM, :], preferred_element_type=F32)
            acc = acc + wit_ref[h:h + 1, :] * jnp.maximum(s, 0.0)
        return acc

    key_i = lax.broadcasted_iota(I32, (t, t), 0)
    qry_i = lax.broadcasted_iota(I32, (t, t), 1)
    causal = key_i <= qry_i

    def store_tile(c, acc):
        sc_ref[c] = acc
        key = _f32_to_key(acc)
        khi_ref[c] = (key >> HALF_BITS).astype(I16)
        klo_ref[c] = ((key & LOW_MASK) - LOW_BIAS).astype(I16)

    diag = jnp.where(causal, score_tile(j), NEG)
    store_tile(j, diag)
    gmax_ref[...] = diag

    def far_score(c, carry):
        acc = score_tile(c)
        store_tile(c, acc)
        gmax_ref[...] = jnp.maximum(gmax_ref[...], acc)
        return carry

    lax.fori_loop(0, j, far_score, 0)

    @pl.when((j + 1) % 2 == 1)
    def _():
        khi_ref[j + 1] = jnp.full((t, t), I16_MIN, I16)
        klo_ref[j + 1] = jnp.full((t, t), I16_MIN, I16)

    n_pairs = (j + 2) // 2
    packed = 2 * SUBLANES

    def count_ge(plane_ref, cand):
        cand = jnp.broadcast_to(cand, (packed, t)).astype(I16)
        one = jnp.ones((packed, t), I16)

        def count_pair(pi, cnts):
            cnts = list(cnts)
            for u in range(2):
                for k in range(t // packed):
                    slab = plane_ref[2 * pi + u, k * packed:(k + 1) * packed, :]
                    a = k % len(cnts)
                    cnts[a] = jnp.where(slab >= cand, cnts[a] + one, cnts[a])
            return tuple(cnts)

        cnts = lax.fori_loop(0, n_pairs, count_pair, tuple(jnp.zeros((packed, t), I16) for _ in range(4)))
        total = (cnts[0].astype(I32) + cnts[1].astype(I32)) + (cnts[2].astype(I32) + cnts[3].astype(I32))
        return jnp.sum(total, axis=0, keepdims=True).astype(F32)

    def bisect(plane_ref, want):
        def step(it, state):
            lo, hi, n_lo, n_hi = state
            mid = lo + ((hi - lo) >> 1)
            total = count_ge(plane_ref, mid)
            ge = total >= want
            return (jnp.where(ge, mid, lo), jnp.where(ge, hi, mid),
                    jnp.where(ge, total, n_lo), jnp.where(ge, n_hi, total))
        return step

    unknown = jnp.full((1, t), float(2 ** 30), F32)
    zero = jnp.zeros((1, t), F32)
    gm = gmax_ref[...]
    hi_lo0 = _f32_to_key(jnp.min(gm, axis=0, keepdims=True)) >> HALF_BITS
    hi_hi0 = (_f32_to_key(jnp.max(gm, axis=0, keepdims=True)) >> HALF_BITS) + 1
    steps = jnp.max(32 - lax.clz(hi_hi0 - hi_lo0 - 1))
    top, _, _, n_above = lax.fori_loop(0, steps, bisect(khi_ref, float(topk)), (hi_lo0, hi_hi0, unknown, zero))

    top16 = jnp.broadcast_to(top, (t, t)).astype(I16)

    def keep_bucket(c, carry):
        klo_ref[c] = jnp.where(khi_ref[c] == top16, klo_ref[c], I16_MIN)
        return carry

    lax.fori_loop(0, j + 1, keep_bucket, 0)
    need = float(topk) - n_above
    lo_start = jnp.full((1, t), I16_MIN, I32)
    lo_stop = jnp.full((1, t), -I16_MIN, I32)
    low, _, n_low, _ = lax.fori_loop(0, HALF_BITS, bisect(klo_ref, need), (lo_start, lo_stop, unknown, zero))
    thr = _key_to_f32((top << HALF_BITS) + (low + LOW_BIAS))
    n_lo = n_above + n_low

    @pl.when(jnp.max(n_lo) > float(topk))
    def _():
        def tile_count(pred):
            def body(c, acc):
                hit = pred(sc_ref[c], key_i + c * t)
                return acc + jnp.sum(jnp.where(hit, 1.0, 0.0), axis=0, keepdims=True)
            return lax.fori_loop(0, j + 1, body, jnp.zeros((1, t), F32))

        surplus = tile_count(lambda s, i: s >= thr) > float(topk)
        need = float(topk) - tile_count(lambda s, i: s > thr)

        def cut_step(it, state):
            below, upto = state
            mid = (below + upto) >> 1
            ok = tile_count(lambda s, i: (s == thr) & (i <= mid)) >= need
            return jnp.where(ok, below, mid), jnp.where(ok, mid, upto)

        last = (j + 1) * t - 1
        _, cutoff = lax.fori_loop(0, idx_steps, cut_step,
                                  (jnp.full((1, t), -1, I32), jnp.full((1, t), last, I32)))

        def demote(c, carry):
            s = sc_ref[c]
            drop = surplus & (s == thr) & (key_i + c * t > cutoff)
            sc_ref[c] = jnp.where(drop, NEG, s)
            return carry

        lax.fori_loop(0, j + 1, demote, 0)

    for g in range(N_KV):
        for r in range(GROUP):
            h = g * GROUP + r
            qg_ref[g, r * t:(r + 1) * t, :] = q_ref[:, h * HEAD_DIM:(h + 1) * HEAD_DIM]
    m_ref[...] = jnp.full(m_ref.shape, NEG, F32)
    l_ref[...] = jnp.zeros(l_ref.shape, F32)
    acc_ref[...] = jnp.zeros(acc_ref.shape, F32)
    def bias_rows(h, kind, q0, rc):
        upper, off = divmod(q0, LANES)
        d_blk = tab_ref[h, 0, off:off + rc, :]
        p_blk = tab_ref[h, 1, off:off + rc, :]
        zero_blk = jnp.zeros((rc, LANES), F32)
        own = [p_blk, d_blk] if upper else [d_blk, zero_blk]
        if kind == "diag":
            return jnp.concatenate(own, axis=1)
        return jnp.concatenate([zero_blk, zero_blk if upper else p_blk] + own, axis=1)

    def attend(c, ntile, kind):
        width = ntile * t
        rc = (LANES * t) // width // 2
        start = pl.multiple_of(c * t, t)
        caps = []
        for u in range(ntile):
            cap_kq = jnp.where(sc_ref[c + u] >= thr, F32_MAX, NEG)
            if kind != "far" and u == ntile - 1:
                cap_kq = jnp.where(causal, cap_kq, NEG)
            caps.append(cap_kq.T)
        cap = jnp.concatenate(caps, axis=1) if ntile > 1 else caps[0]
        def logits(g):
            kt = k_ref[pl.ds(start, width), g * HEAD_DIM:(g + 1) * HEAD_DIM]
            return lax.dot_general(qg_ref[g], kt, (((1,), (1,)), ((), ())), preferred_element_type=F32)

        s_next = logits(0)
        for g in range(N_KV):
            lo = g * HEAD_DIM
            vt = v_ref[pl.ds(start, width), lo:lo + HEAD_DIM]
            s_all = s_next
            if g + 1 < N_KV:
                s_next = logits(g + 1)
            probs, alphas = [], []
            for ch in range(rows // rc):
                rs = slice(ch * rc, (ch + 1) * rc)
                r, q0 = divmod(ch * rc, t)
                s = s_all[rs]
                if kind != "far":
                    s = s + bias_rows(g * GROUP + r, kind, q0, rc)
                s = jnp.minimum(s, cap[q0:q0 + rc])
                m_prev = m_ref[g, rs]
                m_next = jnp.maximum(m_prev, jnp.max(s, axis=1, keepdims=True))
                alpha = jnp.exp2(m_prev - m_next)
                p = jnp.exp2(s - jnp.concatenate([m_next] * (width // LANES), axis=1))
                psum = p[:, :LANES]
                for w in range(1, width // LANES):
                    psum = psum + p[:, w * LANES:(w + 1) * LANES]
                l_ref[g, rs] = alpha * l_ref[g, rs] + psum
                m_ref[g, rs] = m_next
                probs.append(p.astype(BF16))
                alphas.append(alpha)
            pv = jnp.dot(jnp.concatenate(probs, axis=0), vt, preferred_element_type=F32)
            acc_ref[g] = jnp.concatenate(alphas, axis=0) * acc_ref[g] + pv

    n_far = jnp.maximum(j - 1, 0)

    def far_pair(c2, carry):
        attend(2 * c2, 2, "far")
        return carry

    lax.fori_loop(0, n_far // 2, far_pair, 0)

    @pl.when(n_far % 2 == 1)
    def _():
        attend(n_far - 1, 1, "far")

    @pl.when(j >= 1)
    def _():
        attend(j - 1, 2, "near")

    @pl.when(j == 0)
    def _():
        attend(0, 1, "diag")
    for g in range(N_KV):
        out = acc_ref[g] / jnp.sum(l_ref[g], axis=1, keepdims=True)
        for r in range(GROUP):
            h = g * GROUP + r
            o_ref[:, h * HEAD_DIM:(h + 1) * HEAD_DIM] = out[r * t:(r + 1) * t].astype(o_ref.dtype)


def dsa_attention(proj, qit, wit, tab, batch, topk):
    tt = proj.shape[0]
    s = tt // batch
    t = DSA_T
    nq = s // t
    a_q = N_HEADS * HEAD_DIM
    a_kv = N_KV * HEAD_DIM
    a_qi = IDX_HEADS * IDX_DIM
    k_blk = a_q // a_kv
    ki_blk = (a_q + 2 * a_kv) // LANES
    rows = GROUP * t
    return pl.pallas_call(
        functools.partial(_dsa_kernel, topk=topk, idx_steps=math.ceil(math.log2(s + 1))),
        out_shape=jax.ShapeDtypeStruct((tt, a_q), BF16),
        grid=(batch, nq),
        in_specs=[
            pl.BlockSpec((t, a_q), lambda b, j: (b * nq + j, 0)),
            _resident((s, a_kv), lambda b, j: (b, k_blk)),
            _resident((s, a_kv), lambda b, j: (b, k_blk + 1)),
            pl.BlockSpec((a_qi, t), lambda b, j: (0, b * nq + j)),
            _resident((s, LANES), lambda b, j: (b, ki_blk)),
            pl.BlockSpec((IDX_HEADS, t), lambda b, j: (0, b * nq + j)),
            _resident((N_HEADS, 2, LANES, LANES), lambda b, j: (0, 0, 0, 0)),
        ],
        out_specs=pl.BlockSpec((t, a_q), lambda b, j: (b * nq + j, 0)),
        scratch_shapes=[
            pltpu.VMEM((nq, t, t), F32),
            pltpu.VMEM((nq, t, t), I16),
            pltpu.VMEM((nq, t, t), I16),
            pltpu.VMEM((t, t), F32),
            pltpu.VMEM((N_KV, rows, HEAD_DIM), BF16),
            pltpu.VMEM((N_KV, rows, LANES), F32),
            pltpu.VMEM((N_KV, rows, LANES), F32),
            pltpu.VMEM((N_KV, rows, HEAD_DIM), F32),
        ],
        compiler_params=_params("parallel", "arbitrary"),
        name="dsa_attention",
    )(proj, proj, proj, qit, proj, wit, tab)


def kernel(x, p, w_in_a, w_out_a, w_q_b, w_out_b, sinks, g_kv, w_kv, rel_bias,
           g_attn, g_ffn, w1, w3, w2, g_pe, w_pe, w_pg, g_final):
    batch, seq, d = x.shape
    depth = g_attn.shape[0]
    n_a = w_in_a.shape[0]
    tt = batch * seq
    a_q = N_HEADS * HEAD_DIM
    a_kv = N_KV * HEAD_DIM
    a_qi = IDX_HEADS * IDX_DIM
    n_main = a_q + 2 * a_kv + a_qi
    topk = min(TOPK_MAX, seq // 4)
    assert DSA_T == 2 * LANES and seq % (2 * DSA_T) == 0 and topk <= DSA_T and tt % TM == 0

    attn_scale = HEAD_DIM ** -0.5
    log2e = math.log2(math.e)
    n_row = a_q + 2 * a_kv + LANES
    scale_row = jnp.concatenate([
        jnp.full((a_q,), attn_scale * log2e, F32), jnp.ones((n_row - a_q,), F32)])
    scale_q = jnp.full((a_q,), attn_scale, F32)

    def bias_table(dist, values):
        onehot = jnp.asarray(_t5_bucket_np(dist))[..., None] == jnp.arange(N_BUCKETS)
        return jnp.einsum("...b,bh->h...", onehot.astype(F32), values, precision=lax.Precision.HIGHEST)

    r = np.arange(LANES)
    dist_diag = r[:, None] - r[None, :]
    dist_prev = dist_diag + LANES
    far_bucket = N_BUCKETS - 1
    assert (_t5_bucket_np(np.arange(LANES, 4 * LANES)) == far_bucket).all()
    rb = rel_bias.astype(F32)
    rb_shift = (rb - rb[far_bucket][None, :]) * log2e
    dsa_tab = bias_table(np.stack([dist_diag, dist_prev]), rb_shift)
    swa_tab = bias_table(r[:, None] + SWA_TQ - np.arange(2 * SWA_TQ)[None, :], rb)

    n_qkv = a_q + 2 * a_kv
    w_in_row = jnp.concatenate([
        w_in_a[:, :, :n_qkv], w_in_a[:, :, n_main:n_main + IDX_DIM],
        jnp.zeros(w_in_a.shape[:2] + (LANES - IDX_DIM,), w_in_a.dtype)], axis=2).astype(BF16)
    w_in_t = jnp.concatenate([
        w_in_a[:, :, n_qkv:n_main], w_in_a[:, :, n_main + IDX_DIM:n_main + IDX_DIM + IDX_HEADS]],
        axis=2).transpose(0, 2, 1).astype(BF16)
    w_out_a_b, w_q_b_b, w_out_b_b, w_kv_b = (a.astype(BF16) for a in (w_out_a, w_q_b, w_out_b, w_kv))
    w1_b, w3_b, w2_b, w_pg_b, w_pe_b = (a.astype(BF16) for a in (w1, w3, w2, w_pg, w_pe))

    x2 = x.reshape(tt, d)
    p2 = p.reshape(depth, tt, p.shape[-1])
    kv = None
    for i in range(depth):
        if i < n_a:
            proj, qit, wit = dsa_project(x2, g_attn[i], w_in_row, scale_row, w_in_t, i, a_qi,
                                         IDX_DIM ** -0.5, IDX_HEADS ** -0.5)
            attn = dsa_attention(proj, qit, wit, dsa_tab, batch, topk)
            x2 = linear_residual(attn, w_out_a_b, i, x2)
        else:
            jb = i - n_a
            q = norm_linear(x2, g_attn[i], w_q_b_b, jb, scale_q, BF16)
            attn = swa_attention(q, kv, sinks[jb], swa_tab, batch)
            x2 = linear_residual(attn, w_out_b_b, jb, x2)
        x2 = ffn(x2, g_ffn[i], w1_b, w3_b, w2_b, i)
        x2 = ple_gate(x2, g_pe[i], w_pg_b, p2, w_pe_b, i, g_final if i == depth - 1 else None)
        if i == n_a - 1:
            kv = norm_linear(x2, g_kv, w_kv_b, None, None, BF16)
    return x2.reshape(batch, seq, d)
```
